```python
import math
import jax, jax.numpy as jnp
from jax import lax
import numpy as np

D_MODEL = 2048
BATCH = 16
SEQ = 256
DEPTH = 4
DEC_BATCH = 4
DEC_SEQ = 2048
PAST_LEN = 512

GRID_W = 64
RMS_EPS = 1e-6
ROPE_BASE = 10000.0
NEG_INF = -1e30
Q_BLOCK = 128
N_MOD = 6
MLA_HEADS = 16
MLA_Q_LORA = 512
MLA_KV_LORA = 256
MLA_NOPE = 128
MLA_ROPE = 64
MLA_V = 128
MLA_QK_DIM = MLA_NOPE + MLA_ROPE
MLA_SCALE = 1.0 / math.sqrt(MLA_QK_DIM)
SWA_Q_HEADS = 32
SWA_KV_HEADS = 4
SWA_GROUP = SWA_Q_HEADS // SWA_KV_HEADS
SWA_HEAD_DIM = 64
SWA_WINDOW = 128
SWA_BLOCK = 128
SWA_SCALE = 1.0 / math.sqrt(SWA_HEAD_DIM)
N_EXPERTS = 16
D_EXPERT = 1024
EC_CAPACITY = 2
N_MLA = (DEPTH + 1) // 2
N_SWA = DEPTH // 2

kernel_name = 'hybrid_mla_swa_ec_diffusion_step'


def rmsnorm(x, g):
    xf = x.astype(jnp.float32)
    y = xf * lax.rsqrt(jnp.mean(xf * xf, axis=-1, keepdims=True) + RMS_EPS)
    return (y * g.astype(jnp.float32)).astype(x.dtype)


def adaln(cvec, w, b):
    m = jax.nn.silu(cvec) @ w + b
    return [m[:, None, k * D_MODEL:(k + 1) * D_MODEL] for k in range(N_MOD)]


def modulate(h, shift, scale):
    return h * (1 + scale) + shift


def grid_rope(L, rot_dim):
    rows = L // GRID_W
    row = jnp.repeat(jnp.arange(rows), GRID_W).astype(jnp.float32)
    col = jnp.tile(jnp.arange(GRID_W), rows).astype(jnp.float32)
    nf = rot_dim // 4
    inv = ROPE_BASE ** (-jnp.arange(nf, dtype=jnp.float32) / nf)
    ang = jnp.concatenate([row[:, None] * inv[None], col[:, None] * inv[None]], axis=-1)
    return jnp.cos(ang), jnp.sin(ang)


def apply_rope(x, cos, sin):
    half = x.shape[-1] // 2
    x1, x2 = x[..., :half], x[..., half:]
    c = cos[None, :, None, :].astype(x.dtype)
    s = sin[None, :, None, :].astype(x.dtype)
    return jnp.concatenate([x1 * c - x2 * s, x1 * s + x2 * c], axis=-1)


def softmax_attend(q, k, v, scale, sink, mask):
    s = jnp.einsum('bqhgd,bkhd->bhgqk', q, k, preferred_element_type=jnp.float32) * scale
    if mask is not None:
        s = jnp.where(mask, s, NEG_INF)
    if sink is not None:
        col = jnp.broadcast_to(sink.astype(jnp.float32)[None, :, :, None, None], s.shape[:-1] + (1,))
        p = jax.nn.softmax(jnp.concatenate([s, col], axis=-1), axis=-1)[..., :-1]
    else:
        p = jax.nn.softmax(s, axis=-1)
    return jnp.einsum('bhgqk,bkhd->bqhgd', p.astype(v.dtype), v)


def attend_blocks(q, k, v, scale, sink):
    B, L = q.shape[:2]
    nb = L // Q_BLOCK
    qb = q.reshape((B, nb, Q_BLOCK) + q.shape[2:]).swapaxes(0, 1)
    o = lax.map(lambda qi: softmax_attend(qi, k, v, scale, sink, None), qb)
    return o.swapaxes(0, 1).reshape((B, L) + o.shape[3:])


def mla_project(h, w_in, q_norm, w_uq, kv_norm):
    B, L, _ = h.shape
    a = h @ w_in
    cq = a[..., :MLA_Q_LORA]
    ckv = a[..., MLA_Q_LORA:MLA_Q_LORA + MLA_KV_LORA]
    kpe = a[..., MLA_Q_LORA + MLA_KV_LORA:]
    q = (rmsnorm(cq, q_norm) @ w_uq).reshape(B, L, MLA_HEADS, MLA_QK_DIM)
    return q, rmsnorm(ckv, kv_norm), kpe


def mla_expand(ckv, kpe, w_ukv):
    B, L, _ = ckv.shape
    kv = (ckv @ w_ukv).reshape(B, L, MLA_HEADS, MLA_NOPE + MLA_V)
    k = jnp.concatenate([kv[..., :MLA_NOPE],
                         jnp.broadcast_to(kpe[:, :, None, :], (B, L, MLA_HEADS, MLA_ROPE))], axis=-1)
    return k, kv[..., MLA_NOPE:]


def mla_context(h, w_in, q_norm, w_uq, kv_norm, w_ukv, w_o):
    B, L, _ = h.shape
    q, ckv, kpe = mla_project(h, w_in, q_norm, w_uq, kv_norm)
    k, v = mla_expand(ckv, kpe, w_ukv)
    o = attend_blocks(q[:, :, :, None], k, v, MLA_SCALE, None)
    return o.reshape(B, L, MLA_HEADS * MLA_V) @ w_o, ckv, kpe


def mla_latent(h, ckv_ctx, kpe_ctx, cos, sin, w_in, q_norm, w_uq, kv_norm, w_ukv, w_o):
    B, L, _ = h.shape
    q, ckv, kpe = mla_project(h, w_in, q_norm, w_uq, kv_norm)
    q = jnp.concatenate([q[..., :MLA_NOPE], apply_rope(q[..., MLA_NOPE:], cos, sin)], axis=-1)
    kpe = apply_rope(kpe[:, :, None, :], cos, sin)[:, :, 0]
    k, v = mla_expand(jnp.concatenate([ckv_ctx, ckv], axis=1),
                      jnp.concatenate([kpe_ctx, kpe], axis=1), w_ukv)
    o = attend_blocks(q[:, :, :, None], k, v, MLA_SCALE, None)
    return o.reshape(B, L, MLA_HEADS * MLA_V) @ w_o


def swa_split(h, w_qkv):
    B, L, _ = h.shape
    qkv = h @ w_qkv
    nq = SWA_Q_HEADS * SWA_HEAD_DIM
    nk = SWA_KV_HEADS * SWA_HEAD_DIM
    q = qkv[..., :nq].reshape(B, L, SWA_Q_HEADS, SWA_HEAD_DIM)
    k = qkv[..., nq:nq + nk].reshape(B, L, SWA_KV_HEADS, SWA_HEAD_DIM)
    v = qkv[..., nq + nk:].reshape(B, L, SWA_KV_HEADS, SWA_HEAD_DIM)
    return q, k, v


def swa_context(h, w_qkv, sink, w_o):
    B, L, _ = h.shape
    q, k, v = swa_split(h, w_qkv)
    q = q.reshape(B, L, SWA_KV_HEADS, SWA_GROUP, SWA_HEAD_DIM)
    o = attend_blocks(q, k, v, SWA_SCALE, sink.reshape(SWA_KV_HEADS, SWA_GROUP))
    return o.reshape(B, L, SWA_Q_HEADS * SWA_HEAD_DIM) @ w_o, k, v


def swa_latent(h, k_ctx, v_ctx, cos, sin, w_qkv, sink, w_o):
    B, L, _ = h.shape
    Lc = k_ctx.shape[1]
    q, k, v = swa_split(h, w_qkv)
    q = apply_rope(q, cos, sin).reshape(B, L, SWA_KV_HEADS, SWA_GROUP, SWA_HEAD_DIM)
    k = apply_rope(k, cos, sin)
    nb = L // SWA_BLOCK
    pad = ((0, 0), (SWA_BLOCK, SWA_BLOCK), (0, 0), (0, 0))
    kp = jnp.pad(k, pad)
    vp = jnp.pad(v, pad)
    qb = q.reshape(B, nb, SWA_BLOCK, SWA_KV_HEADS, SWA_GROUP, SWA_HEAD_DIM).swapaxes(0, 1)
    sink_g = sink.reshape(SWA_KV_HEADS, SWA_GROUP)
    qoff = jnp.arange(SWA_BLOCK)
    koff = jnp.arange(3 * SWA_BLOCK)
    ctx_mask = jnp.ones((SWA_BLOCK, Lc), dtype=bool)

    def block(args):
        qi, i = args
        start = i * SWA_BLOCK
        kb = lax.dynamic_slice_in_dim(kp, start, 3 * SWA_BLOCK, axis=1)
        vb = lax.dynamic_slice_in_dim(vp, start, 3 * SWA_BLOCK, axis=1)
        qpos = start + qoff
        kpos = start - SWA_BLOCK + koff
        band = ((jnp.abs(qpos[:, None] - kpos[None, :]) <= SWA_WINDOW)
                & (kpos >= 0)[None, :] & (kpos < L)[None, :])
        mask = jnp.concatenate([ctx_mask, band], axis=1)
        return softmax_attend(qi, jnp.concatenate([k_ctx, kb], axis=1),
                              jnp.concatenate([v_ctx, vb], axis=1), SWA_SCALE, sink_g, mask)

    o = lax.map(block, (qb, jnp.arange(nb)))
    o = o.swapaxes(0, 1).reshape(B, L, SWA_Q_HEADS * SWA_HEAD_DIM)
    return o @ w_o


def expert_choice_ffn(h, w_router, w_gate, w_up, w_down):
    B, L, D = h.shape
    n_tok = B * L
    cap = EC_CAPACITY * n_tok // N_EXPERTS
    xt = h.reshape(n_tok, D)
    aff = jax.nn.softmax((xt @ w_router).astype(jnp.float32), axis=-1)
    g, idx = lax.top_k(aff.T, cap)
    xe = xt[idx]
    hid = jax.nn.silu(jnp.einsum('ecd,edf->ecf', xe, w_gate)) * jnp.einsum('ecd,edf->ecf', xe, w_up)
    ye = jnp.einsum('ecf,efd->ecd', hid, w_down) * g[..., None].astype(xe.dtype)
    out = jnp.zeros_like(xt).at[idx.reshape(-1)].add(ye.reshape(-1, D))
    return out.reshape(B, L, D)


def setup_inputs(seed: int = 0) -> dict:
    key = jax.random.key(seed)
    ks = jax.random.split(key, 32)
    f32 = jnp.float32
    D = D_MODEL

    def nrm(k, shape, scale):
        return jax.random.normal(k, shape, f32) * scale

    return {
        'x_prompt': nrm(ks[0], (BATCH, SEQ, D), 1.0),
        'x_sample': nrm(ks[1], (DEC_BATCH, DEC_SEQ, D), 1.0),
        'cache_ckv': nrm(ks[2], (DEC_BATCH, N_MLA, PAST_LEN, MLA_KV_LORA), 1.0),
        'cache_kpe': nrm(ks[3], (DEC_BATCH, N_MLA, PAST_LEN, MLA_ROPE), 1.0),
        'cache_k': nrm(ks[4], (DEC_BATCH, N_SWA, PAST_LEN, SWA_KV_HEADS, SWA_HEAD_DIM), 1.0),
        'cache_v': nrm(ks[5], (DEC_BATCH, N_SWA, PAST_LEN, SWA_KV_HEADS, SWA_HEAD_DIM), 1.0),
        'c': nrm(ks[6], (DEC_BATCH, D), 1.0),
        'c_ctx': nrm(ks[7], (D,), 1.0),
        'ada_w': nrm(ks[8], (DEPTH, D, N_MOD * D), 0.5 * D ** -0.5),
        'ada_b': nrm(ks[9], (DEPTH, N_MOD * D), 0.02),
        'norm_mix': 1.0 + nrm(ks[10], (DEPTH, D), 0.05),
        'norm_ffn': 1.0 + nrm(ks[11], (DEPTH, D), 0.05),
        'mla_w_in': nrm(ks[12], (N_MLA, D, MLA_Q_LORA + MLA_KV_LORA + MLA_ROPE), D ** -0.5),
        'mla_q_norm': 1.0 + nrm(ks[13], (N_MLA, MLA_Q_LORA), 0.05),
        'mla_w_uq': nrm(ks[14], (N_MLA, MLA_Q_LORA, MLA_HEADS * MLA_QK_DIM), MLA_Q_LORA ** -0.5),
        'mla_kv_norm': 1.0 + nrm(ks[15], (N_MLA, MLA_KV_LORA), 0.05),
        'mla_w_ukv': nrm(ks[16], (N_MLA, MLA_KV_LORA, MLA_HEADS * (MLA_NOPE + MLA_V)), MLA_KV_LORA ** -0.5),
        'mla_w_o': nrm(ks[17], (N_MLA, MLA_HEADS * MLA_V, D), (MLA_HEADS * MLA_V) ** -0.5),
        'swa_w_qkv': nrm(ks[18], (N_SWA, D, (SWA_Q_HEADS + 2 * SWA_KV_HEADS) * SWA_HEAD_DIM), D ** -0.5),
        'swa_sink': nrm(ks[19], (N_SWA, SWA_Q_HEADS), 1.0),
        'swa_w_o': nrm(ks[20], (N_SWA, SWA_Q_HEADS * SWA_HEAD_DIM, D), (SWA_Q_HEADS * SWA_HEAD_DIM) ** -0.5),
        'moe_router': nrm(ks[21], (DEPTH, D, N_EXPERTS), D ** -0.5),
        'moe_w_gate': nrm(ks[22], (DEPTH, N_EXPERTS, D, D_EXPERT), D ** -0.5),
        'moe_w_up': nrm(ks[23], (DEPTH, N_EXPERTS, D, D_EXPERT), D ** -0.5),
        'moe_w_down': nrm(ks[24], (DEPTH, N_EXPERTS, D_EXPERT, D), D_EXPERT ** -0.5),
        'final_norm': 1.0 + nrm(ks[25], (D,), 0.05),
    }


def reference(x_prompt, x_sample, cache_ckv, cache_kpe, cache_k, cache_v, c, c_ctx,
              ada_w, ada_b, norm_mix, norm_ffn,
              mla_w_in, mla_q_norm, mla_w_uq, mla_kv_norm, mla_w_ukv, mla_w_o,
              swa_w_qkv, swa_sink, swa_w_o,
              moe_router, moe_w_gate, moe_w_up, moe_w_down, final_norm):
    L_lat = x_sample.shape[1]
    cos_mla, sin_mla = grid_rope(L_lat, MLA_ROPE)
    cos_swa, sin_swa = grid_rope(L_lat, SWA_HEAD_DIM)
    xp, xs = x_prompt, x_sample
    new_ckv, new_kpe, new_k, new_v = [], [], [], []
    for i in range(DEPTH):
        j = i // 2
        mp = adaln(c_ctx[None, :], ada_w[i], ada_b[i])
        ms = adaln(c, ada_w[i], ada_b[i])
        hp = modulate(rmsnorm(xp, norm_mix[i]), mp[0], mp[1])
        hs = modulate(rmsnorm(xs, norm_mix[i]), ms[0], ms[1])
        if i % 2 == 0:
            w = (mla_w_in[j], mla_q_norm[j], mla_w_uq[j], mla_kv_norm[j], mla_w_ukv[j], mla_w_o[j])
            op, ckv, kpe = mla_context(hp, *w)
            os_ = mla_latent(hs, cache_ckv[:, j], cache_kpe[:, j], cos_mla, sin_mla, *w)
            new_ckv.append(ckv)
            new_kpe.append(kpe)
        else:
            w = (swa_w_qkv[j], swa_sink[j], swa_w_o[j])
            op, kc, vc = swa_context(hp, *w)
            os_ = swa_latent(hs, cache_k[:, j], cache_v[:, j], cos_swa, sin_swa, *w)
            new_k.append(kc)
            new_v.append(vc)
        xp = xp + mp[2] * op
        xs = xs + ms[2] * os_
        moe = (moe_router[i], moe_w_gate[i], moe_w_up[i], moe_w_down[i])
        hp = modulate(rmsnorm(xp, norm_ffn[i]), mp[3], mp[4])
        hs = modulate(rmsnorm(xs, norm_ffn[i]), ms[3], ms[4])
        xp = xp + mp[5] * expert_choice_ffn(hp, *moe)
        xs = xs + ms[5] * expert_choice_ffn(hs, *moe)
    y_prompt = rmsnorm(xp, final_norm)
    y_sample = rmsnorm(xs, final_norm)
    return (y_prompt, y_sample, jnp.stack(new_ckv, axis=1), jnp.stack(new_kpe, axis=1),
            jnp.stack(new_k, axis=1), jnp.stack(new_v, axis=1))
```

```python
import functools
import math

import jax
import jax.numpy as jnp
from jax import lax
from jax.experimental import pallas as pl
from jax.experimental.pallas import tpu as pltpu

F32 = jnp.float32
BF16 = jnp.bfloat16
I32 = jnp.int32

GRID_W = 64
RMS_EPS = 1e-6
ROPE_BASE = 10000.0
NEG_INF = -1e30
N_MOD = 6
MLA_NOPE = 128
MLA_ROPE = 64
MLA_V = 128
SWA_HEAD_DIM = 64
SWA_WINDOW = 128
SWA_BLOCK = 128
EC_CAPACITY = 2

LANES = 128
TM = 256
CHUNK = 128
EXT = 128
VMEM_LIMIT = 52 * 1024 * 1024


def _cparams(*sem):
    return pltpu.CompilerParams(dimension_semantics=sem, vmem_limit_bytes=VMEM_LIMIT)


def _rms(x, w):
    return x * lax.rsqrt(jnp.mean(x * x, axis=-1, keepdims=True) + RMS_EPS) * w


def _swap_halves(x):
    w = x.shape[-1]
    lane = lax.broadcasted_iota(I32, x.shape, x.ndim - 1)
    return jnp.where(lane % 64 < 32, pltpu.roll(x, w - 32, x.ndim - 1), pltpu.roll(x, 32, x.ndim - 1))


def _rope(x, cos, sin):
    n = x.shape[-1] // LANES
    cw = jnp.concatenate([cos] * n, axis=-1) if n > 1 else cos
    sw = jnp.concatenate([sin] * n, axis=-1) if n > 1 else sin
    return x * cw + _swap_halves(x) * sw


def _dot(a, b):
    return jnp.dot(a, b, preferred_element_type=F32)


def _dot_nt(a, b):
    return lax.dot_general(a, b, (((1,), (1,)), ((), ())), preferred_element_type=F32)


def _adaln_kernel(cv_ref, w_ref, b_ref, o_ref):
    a = cv_ref[...]
    a = a * jax.nn.sigmoid(a)
    o_ref[...] = _dot(a.astype(BF16), w_ref[...].astype(BF16)) + b_ref[...]


def _adaln(cv, ada_w, ada_b):
    depth, d, n6 = ada_w.shape
    tn = d // 2
    return pl.pallas_call(
        _adaln_kernel,
        grid=(depth, n6 // tn),
        in_specs=[
            pl.BlockSpec((8, d), lambda l, j: (0, 0)),
            pl.BlockSpec((None, d, tn), lambda l, j: (l, 0, j)),
            pl.BlockSpec((None, 1, tn), lambda l, j: (l, 0, j)),
        ],
        out_specs=pl.BlockSpec((None, 8, tn), lambda l, j: (l, 0, j)),
        out_shape=jax.ShapeDtypeStruct((depth, 8, n6), F32),
        compiler_params=_cparams("arbitrary", "arbitrary"),
        name="adaln",
    )(cv, ada_w, ada_b.reshape(depth, 1, n6))


class _Rows:
    def __init__(self, n_prompt, n_sample, dec_seq):
        self.n_prompt, self.n_sample, self.dec_seq = n_prompt, n_sample, dec_seq
        self.n = n_prompt + n_sample
        assert n_prompt % TM == 0 and dec_seq % TM == 0
        self.tiles = self.n // TM
        self.ptiles = n_prompt // TM
        self.tiles_per_seq = dec_seq // TM

    def mod_row(self, i):
        return jnp.where(i < self.ptiles, 0, 1 + (i - self.ptiles) // self.tiles_per_seq)

    def rope_block(self, i):
        return jnp.where(i < self.ptiles, 0, 1 + (i - self.ptiles) % self.tiles_per_seq)

    def mod_spec(self, d, layer, k):
        base = (layer * N_MOD + k) * 8
        return pl.BlockSpec((None, 1, d), lambda i: (base + self.mod_row(i), 0, 0))

    def rope_spec(self):
        return pl.BlockSpec((TM, LANES), lambda i: (self.rope_block(i), 0))


def _row_spec(width):
    return pl.BlockSpec((TM, width), lambda i: (i, 0))


def _full_spec(shape):
    nd = len(shape)
    return pl.BlockSpec(shape, lambda i: (0,) * nd)


def _mla_proj_kernel(x_ref, nw_ref, sh_ref, sc_ref, win_ref, qn_ref, wuq_ref, kvn_ref, cos_ref, sin_ref,
                     q_ref, kvl_ref, *, q_lora, kv_lora, heads):
    x = x_ref[...]
    h = _rms(x, nw_ref[...]) * (1.0 + sc_ref[...]) + sh_ref[...]
    a = _dot(h.astype(BF16), win_ref[...])
    cq = a[:, :q_lora]
    ckv = a[:, q_lora:q_lora + kv_lora]
    kpe = a[:, q_lora + kv_lora:q_lora + kv_lora + LANES]
    cos, sin = cos_ref[...], sin_ref[...]
    q = _dot(_rms(cq, qn_ref[...]).astype(BF16), wuq_ref[...])
    for hh in range(heads):
        lo = hh * 2 * LANES
        q_ref[:, lo:lo + LANES] = q[:, lo:lo + LANES].astype(BF16)
        q_ref[:, lo + LANES:lo + 2 * LANES] = _rope(q[:, lo + LANES:lo + 2 * LANES], cos, sin).astype(BF16)
    kvl_ref[:, :kv_lora] = _rms(ckv, kvn_ref[...])
    kvl_ref[:, kv_lora:] = _rope(kpe, cos, sin)


def _mla_proj(rows, x, mods, layer, norm_w, w_in_p, q_norm, w_uq_p, kv_norm, cos_t, sin_t):
    d = x.shape[1]
    q_lora, kv_lora = q_norm.shape[-1], kv_norm.shape[-1]
    heads = w_uq_p.shape[1] // (2 * LANES)
    kern = functools.partial(_mla_proj_kernel, q_lora=q_lora, kv_lora=kv_lora, heads=heads)
    return pl.pallas_call(
        kern,
        grid=(rows.tiles,),
        in_specs=[
            _row_spec(d), _full_spec((1, d)), rows.mod_spec(d, layer, 0), rows.mod_spec(d, layer, 1),
            _full_spec(w_in_p.shape), _full_spec((1, q_lora)), _full_spec(w_uq_p.shape),
            _full_spec((1, kv_lora)), rows.rope_spec(), rows.rope_spec(),
        ],
        out_specs=[_row_spec(heads * 2 * LANES), _row_spec(kv_lora + LANES)],
        out_shape=[jax.ShapeDtypeStruct((rows.n, heads * 2 * LANES), BF16),
                   jax.ShapeDtypeStruct((rows.n, kv_lora + LANES), F32)],
        compiler_params=_cparams("arbitrary"),
        name="mla_proj",
    )(x, norm_w.reshape(1, d), mods, mods, w_in_p, q_norm.reshape(1, -1), w_uq_p, kv_norm.reshape(1, -1),
      cos_t, sin_t)


def _mla_attn_kernel(*refs, n_ctx, kv_lora, scale):
    if n_ctx:
        q_ref, cckv_ref, ckpe_ref, kvl_ref, wukv_ref, o_ref, k_s, v_s = refs
    else:
        q_ref, kvl_ref, wukv_ref, o_ref, k_s, v_s = refs

    @pl.when(pl.program_id(2) == 0)
    def _():
        w = wukv_ref[...]
        lat = kvl_ref[...]
        kv = _dot(lat[:, :kv_lora].astype(BF16), w)
        k_s[n_ctx:, :MLA_NOPE] = kv[:, :MLA_NOPE].astype(BF16)
        k_s[n_ctx:, MLA_NOPE:] = lat[:, kv_lora:].astype(BF16)
        v_s[n_ctx:, :] = kv[:, MLA_NOPE:].astype(BF16)
        if n_ctx:
            kvc = _dot(cckv_ref[...].astype(BF16), w)
            k_s[:n_ctx, :MLA_NOPE] = kvc[:, :MLA_NOPE].astype(BF16)
            k_s[:n_ctx, MLA_NOPE:MLA_NOPE + MLA_ROPE] = ckpe_ref[...].astype(BF16)
            k_s[:n_ctx, MLA_NOPE + MLA_ROPE:] = jnp.zeros((n_ctx, LANES - MLA_ROPE), BF16)
            v_s[:n_ctx, :] = kvc[:, MLA_NOPE:].astype(BF16)

    s = _dot_nt(q_ref[...], k_s[...]) * scale
    m = jnp.max(s, axis=-1, keepdims=True)
    p = jnp.exp(s - m)
    l = jnp.sum(p, axis=-1, keepdims=True)
    o_ref[...] = (_dot(p.astype(BF16), v_s[...]) / l).astype(BF16)


def _mla_attn(q, kvl, w_ukv_b, row0, nb, seq, heads, ctx=None):
    kv_lora = kvl.shape[1] - LANES
    tq = TM
    nq = seq // tq
    assert row0 % seq == 0
    seq0, tile0 = row0 // seq, row0 // tq
    n_ctx = 0 if ctx is None else ctx[0].shape[2]
    lk = n_ctx + seq
    scale = 1.0 / math.sqrt(MLA_NOPE + MLA_ROPE)
    in_specs = [pl.BlockSpec((tq, 2 * LANES), lambda b, h, i: (tile0 + b * nq + i, h))]
    args = [q]
    if ctx is not None:
        cckv, ckpe, j = ctx
        in_specs += [pl.BlockSpec((None, None, n_ctx, kv_lora), lambda b, h, i: (b, j, 0, 0)),
                     pl.BlockSpec((None, None, n_ctx, MLA_ROPE), lambda b, h, i: (b, j, 0, 0))]
        args += [cckv, ckpe]
    in_specs += [pl.BlockSpec((seq, kv_lora + LANES), lambda b, h, i: (seq0 + b, 0)),
                 pl.BlockSpec((kv_lora, MLA_NOPE + MLA_V), lambda b, h, i: (0, h))]
    args += [kvl, w_ukv_b]
    kern = functools.partial(_mla_attn_kernel, n_ctx=n_ctx, kv_lora=kv_lora, scale=scale)
    return pl.pallas_call(
        kern,
        grid=(nb, heads, nq),
        in_specs=in_specs,
        out_specs=pl.BlockSpec((tq, MLA_V), lambda b, h, i: (b * nq + i, h)),
        out_shape=jax.ShapeDtypeStruct((nb * seq, heads * MLA_V), BF16),
        scratch_shapes=[pltpu.VMEM((lk, 2 * LANES), BF16), pltpu.VMEM((lk, MLA_V), BF16)],
        compiler_params=_cparams("arbitrary", "arbitrary", "arbitrary"),
        name="mla_attn_ctx" if ctx is None else "mla_attn_lat",
    )(*args)


def _swa_proj_kernel(x_ref, nw_ref, sh_ref, sc_ref, w_ref, cos_ref, sin_ref, q_ref, kv_ref, *, nq, nk):
    x = x_ref[...]
    h = _rms(x, nw_ref[...]) * (1.0 + sc_ref[...]) + sh_ref[...]
    a = _dot(h.astype(BF16), w_ref[...])
    qk = _rope(a[:, :nq + nk], cos_ref[...], sin_ref[...])
    q_ref[...] = qk[:, :nq].astype(BF16)
    kv_ref[:, :nk] = qk[:, nq:]
    kv_ref[:, nk:] = a[:, nq + nk:]


def _swa_proj(rows, x, mods, layer, norm_w, w_qkv_b, nq, nk, cos_t, sin_t):
    d = x.shape[1]
    kern = functools.partial(_swa_proj_kernel, nq=nq, nk=nk)
    return pl.pallas_call(
        kern,
        grid=(rows.tiles,),
        in_specs=[_row_spec(d), _full_spec((1, d)), rows.mod_spec(d, layer, 0), rows.mod_spec(d, layer, 1),
                  _full_spec(w_qkv_b.shape), rows.rope_spec(), rows.rope_spec()],
        out_specs=[_row_spec(nq), _row_spec(2 * nk)],
        out_shape=[jax.ShapeDtypeStruct((rows.n, nq), BF16), jax.ShapeDtypeStruct((rows.n, 2 * nk), F32)],
        compiler_params=_cparams("arbitrary"),
        name="swa_proj",
    )(x, norm_w.reshape(1, d), mods, mods, w_qkv_b, cos_t, sin_t)


def _sink_softmax_pv(s_parts, v_parts, sink):
    m = jnp.maximum(functools.reduce(jnp.maximum, [jnp.max(s, axis=-1, keepdims=True) for s in s_parts]), sink)
    l = jnp.exp(sink - m)
    o = None
    for s, v in zip(s_parts, v_parts):
        p = jnp.exp(s - m)
        l = l + jnp.sum(p, axis=-1, keepdims=True)
        pv = _dot(p.astype(BF16), v)
        o = pv if o is None else o + pv
    return o / l


def _swa_ctx_kernel(sink_ref, q_ref, kv_ref, o_ref, *, hq, hkv):
    dh = SWA_HEAD_DIM
    group = hq // hkv
    scale = 1.0 / math.sqrt(dh)
    q = q_ref[...]
    kv = kv_ref[...].astype(BF16)
    for g in range(hkv):
        k = kv[:, g * dh:(g + 1) * dh]
        v = kv[:, (hkv + g) * dh:(hkv + g + 1) * dh]
        for t in range(group):
            hh = g * group + t
            s = _dot_nt(q[:, hh * dh:(hh + 1) * dh], k) * scale
            o_ref[:, hh * dh:(hh + 1) * dh] = _sink_softmax_pv([s], [v], sink_ref[hh]).astype(BF16)


def _swa_ctx(q, kv, sink, nb, seq, hq, hkv):
    kern = functools.partial(_swa_ctx_kernel, hq=hq, hkv=hkv)
    return pl.pallas_call(
        kern,
        grid=(nb,),
        in_specs=[pl.BlockSpec(memory_space=pltpu.SMEM),
                  pl.BlockSpec((seq, hq * SWA_HEAD_DIM), lambda b: (b, 0)),
                  pl.BlockSpec((seq, 2 * hkv * SWA_HEAD_DIM), lambda b: (b, 0))],
        out_specs=pl.BlockSpec((seq, hq * SWA_HEAD_DIM), lambda b: (b, 0)),
        out_shape=jax.ShapeDtypeStruct((nb * seq, hq * SWA_HEAD_DIM), BF16),
        compiler_params=_cparams("arbitrary"),
        name="swa_attn_ctx",
    )(sink, q, kv)


def _swa_lat_kernel(sink_ref, q_ref, kc_ref, vc_ref, kv0_ref, kv1_ref, kv2_ref, o_ref, *, hq, hkv, seq):
    dh = SWA_HEAD_DIM
    group = hq // hkv
    scale = 1.0 / math.sqrt(dh)
    blk = SWA_BLOCK
    i = pl.program_id(1)
    qpos = i * blk + lax.broadcasted_iota(I32, (blk, 3 * blk), 0)
    kpos = (i - 1) * blk + lax.broadcasted_iota(I32, (blk, 3 * blk), 1)
    band = (jnp.abs(qpos - kpos) <= SWA_WINDOW) & (kpos >= 0) & (kpos < seq)
    q = q_ref[...]
    kc = kc_ref[...].astype(BF16)
    vc = vc_ref[...].astype(BF16)
    kvb = jnp.concatenate([kv0_ref[...], kv1_ref[...], kv2_ref[...]], axis=0).astype(BF16)
    for g in range(hkv):
        k_c = kc[:, g * dh:(g + 1) * dh]
        v_c = vc[:, g * dh:(g + 1) * dh]
        k_b = kvb[:, g * dh:(g + 1) * dh]
        v_b = kvb[:, (hkv + g) * dh:(hkv + g + 1) * dh]
        for t in range(group):
            hh = g * group + t
            qh = q[:, hh * dh:(hh + 1) * dh]
            s_c = _dot_nt(qh, k_c) * scale
            s_b = jnp.where(band, _dot_nt(qh, k_b) * scale, NEG_INF)
            o_ref[:, hh * dh:(hh + 1) * dh] = _sink_softmax_pv([s_c, s_b], [v_c, v_b], sink_ref[hh]).astype(BF16)


def _swa_lat(q, kv, sink, cache_k, cache_v, j, row0, nb, seq, hq, hkv):
    blk = SWA_BLOCK
    nblk = seq // blk
    blk0 = row0 // blk
    n_ctx = cache_k.shape[2]
    wq, wkv = hq * SWA_HEAD_DIM, hkv * SWA_HEAD_DIM
    kern = functools.partial(_swa_lat_kernel, hq=hq, hkv=hkv, seq=seq)

    def band_spec(off):
        return pl.BlockSpec((blk, 2 * wkv), lambda b, i: (blk0 + b * nblk + jnp.clip(i + off, 0, nblk - 1), 0))

    return pl.pallas_call(
        kern,
        grid=(nb, nblk),
        in_specs=[pl.BlockSpec(memory_space=pltpu.SMEM),
                  pl.BlockSpec((blk, wq), lambda b, i: (blk0 + b * nblk + i, 0)),
                  pl.BlockSpec((None, None, n_ctx, wkv), lambda b, i: (b, j, 0, 0)),
                  pl.BlockSpec((None, None, n_ctx, wkv), lambda b, i: (b, j, 0, 0)),
                  band_spec(-1), band_spec(0), band_spec(1)],
        out_specs=pl.BlockSpec((blk, wq), lambda b, i: (b * nblk + i, 0)),
        out_shape=jax.ShapeDtypeStruct((nb * seq, wq), BF16),
        compiler_params=_cparams("arbitrary", "arbitrary"),
        name="swa_attn_lat",
    )(sink, q, cache_k, cache_v, kv, kv, kv)


def _attn_out_kernel(o_ref, x_ref, wo_ref, gate_ref, nw_ref, sh_ref, sc_ref, wr_ref, wrt_ref,
                     x1_ref, hx_ref, a2_ref, *, n_exp, ptiles, tiles_per_seq):
    d = x_ref.shape[1]
    x1 = x_ref[...] + gate_ref[...] * _dot(o_ref[...], wo_ref[...])
    x1_ref[...] = x1
    h = _rms(x1, nw_ref[...]) * (1.0 + sc_ref[...]) + sh_ref[...]
    hx_ref[:, :d] = h
    hb = h.astype(BF16)
    lane = lax.broadcasted_iota(I32, (TM, EXT), 1)
    logits = jnp.where(lane < n_exp, _dot(hb, wr_ref[...]), NEG_INF)
    e = jnp.exp(logits - jnp.max(logits, axis=-1, keepdims=True))
    aff = e / jnp.sum(e, axis=-1, keepdims=True)
    i = pl.program_id(0)
    mod_row = jnp.where(i < ptiles, 0, 1 + (i - ptiles) // tiles_per_seq).astype(F32)
    hx_ref[:, d:] = jnp.where(lane == n_exp, mod_row, aff)
    lt = _dot_nt(wrt_ref[...], hb)
    et = jnp.exp(lt - jnp.max(lt, axis=0, keepdims=True))
    at = et / jnp.sum(et, axis=0, keepdims=True)
    a2_ref[...] = jnp.concatenate([at[:, c * CHUNK:(c + 1) * CHUNK] for c in range(TM // CHUNK)], axis=0)


def _attn_out(rows, o, x, w_o_b, mods, layer, norm_w, w_router_p, w_router_t):
    d = x.shape[1]
    n_exp = w_router_t.shape[0]
    kern = functools.partial(_attn_out_kernel, n_exp=n_exp, ptiles=rows.ptiles, tiles_per_seq=rows.tiles_per_seq)
    cpt = TM // CHUNK
    return pl.pallas_call(
        kern,
        grid=(rows.tiles,),
        in_specs=[_row_spec(o.shape[1]), _row_spec(d), _full_spec(w_o_b.shape), rows.mod_spec(d, layer, 2),
                  _full_spec((1, d)), rows.mod_spec(d, layer, 3), rows.mod_spec(d, layer, 4),
                  _full_spec(w_router_p.shape), _full_spec(w_router_t.shape)],
        out_specs=[_row_spec(d), _row_spec(d + EXT), pl.BlockSpec((cpt * n_exp, CHUNK), lambda i: (i, 0))],
        out_shape=[jax.ShapeDtypeStruct((rows.n, d), F32), jax.ShapeDtypeStruct((rows.n, d + EXT), F32),
                   jax.ShapeDtypeStruct((rows.n // CHUNK * n_exp, CHUNK), F32)],
        compiler_params=_cparams("arbitrary"),
        name="attn_out",
    )(o, x, w_o_b, mods, norm_w.reshape(1, d), mods, mods, w_router_p, w_router_t)


def _select_group(a3, cap):
    nj, n_exp, _ = a3.shape
    bits = lax.bitcast_convert_type(a3, I32)
    capf = jnp.float32(cap)

    def count(pred):
        c = jnp.sum(pred.astype(F32), axis=0)
        return jnp.sum(c, axis=-1, keepdims=True)[None]

    def thr_step(k, t):
        cand = t | lax.shift_left(jnp.int32(1), 30 - k)
        return jnp.where(count(bits >= cand) >= capf, cand, t)

    thr = lax.fori_loop(0, 31, thr_step, jnp.zeros((1, n_exp, 1), I32))
    gt = bits > thr
    eq = bits == thr
    need = capf - count(gt)
    tok = (lax.broadcasted_iota(I32, a3.shape, 0) * CHUNK + lax.broadcasted_iota(I32, a3.shape, 2))
    nbits = max(1, (nj * CHUNK).bit_length())

    def tie_step(k, x):
        cand = x | lax.shift_left(jnp.int32(1), nbits - 1 - k)
        return jnp.where(count(eq & (tok < cand)) <= need, cand, x)

    bound = lax.fori_loop(0, nbits, tie_step, jnp.zeros((1, n_exp, 1), I32))
    return (gt | (eq & (tok < bound))).astype(F32)


def _compact(mask_e, cap, tri, lstrict):
    cl = _dot(mask_e.astype(BF16), tri)
    cnt = cl[:, CHUNK - 1:CHUNK]
    off = _dot(lstrict, jnp.broadcast_to(cnt, (LANES, LANES)).astype(BF16))[:, :1]
    ends = off + cnt
    r = lax.broadcasted_iota(I32, (1, cap), 1).astype(F32)
    jstar = jnp.sum((ends <= r).astype(F32), axis=0, keepdims=True)
    onehot = lax.broadcasted_iota(I32, (LANES, cap), 0).astype(F32) == jstar
    offsel = jnp.sum(jnp.where(onehot, off, 0.0), axis=0, keepdims=True)
    g = _dot(cl.T.astype(BF16), onehot.astype(BF16))
    local = jnp.sum((g <= r - offsel).astype(F32), axis=0, keepdims=True)
    return (jstar * CHUNK + local).astype(I32)


def _moe_select_kernel(a2_ref, tri_ref, ls_ref, idx_ref, m_s, *, n_exp, groups):
    tri = tri_ref[...]
    lstrict = ls_ref[...]
    col = 0
    for (chunk0, nj, cap) in groups:
        m_s[...] = jnp.zeros(m_s.shape, F32)
        a3 = a2_ref[chunk0 * n_exp:(chunk0 + nj) * n_exp, :].reshape(nj, n_exp, CHUNK)
        m_s[:nj * n_exp, :] = _select_group(a3, cap).reshape(nj * n_exp, CHUNK)
        for e in range(n_exp):
            mask_e = m_s[pl.ds(e, LANES, stride=n_exp), :]
            idx_ref[e:e + 1, col:col + cap] = _compact(mask_e, cap, tri, lstrict) + chunk0 * CHUNK
        col += cap


def _moe_select(a2, n_exp, groups):
    total = sum(g[2] for g in groups)
    assert all(g[1] <= LANES and g[2] % LANES == 0 for g in groups)
    ii = jnp.arange(LANES)
    tri = (ii[:, None] <= ii[None, :]).astype(BF16)
    lstrict = (ii[None, :] < ii[:, None]).astype(BF16)
    kern = functools.partial(_moe_select_kernel, n_exp=n_exp, groups=groups)
    return pl.pallas_call(
        kern,
        grid=(1,),
        in_specs=[_full_spec(a2.shape), _full_spec((LANES, LANES)), _full_spec((LANES, LANES))],
        out_specs=_full_spec((n_exp, total)),
        out_shape=jax.ShapeDtypeStruct((n_exp, total), I32),
        scratch_shapes=[pltpu.VMEM((LANES * n_exp, CHUNK), F32)],
        compiler_params=_cparams("arbitrary"),
        name="moe_select",
    )(a2, tri, lstrict)


def _moe_ffn_kernel(idx_ref, hx_hbm, x_in_hbm, gates_ref, wg_ref, wu_ref, wd_ref, x_hbm,
                    xs, acc, buf, sem, *, rows_e, rt, d, n_fc):
    del x_in_hbm
    e = pl.program_id(0)
    fc = pl.program_id(1)
    base = e * rows_e

    @pl.when(fc == 0)
    def _():
        def issue(r, c):
            tok = idx_ref[base + r]
            pltpu.make_async_copy(hx_hbm.at[pl.ds(tok, 1), :], xs.at[pl.ds(r, 1), :], sem.at[0]).start()
            return c
        lax.fori_loop(0, rows_e, issue, 0)
        pltpu.make_async_copy(hx_hbm.at[pl.ds(0, rows_e), :], xs, sem.at[0]).wait()

    wg = wg_ref[...].astype(BF16)
    wu = wu_ref[...].astype(BF16)
    wd = wd_ref[...].astype(BF16)
    for t in range(rows_e // rt):
        xb = xs[t * rt:(t + 1) * rt, :d].astype(BF16)
        hid = _dot(xb, wg)
        hid = hid * jax.nn.sigmoid(hid) * _dot(xb, wu)
        part = _dot(hid.astype(BF16), wd)

        @pl.when(fc == 0)
        def _():
            acc[t * rt:(t + 1) * rt, :] = part

        @pl.when(fc > 0)
        def _():
            acc[t * rt:(t + 1) * rt, :] += part

    @pl.when(fc == n_fc - 1)
    def _():
        gates = gates_ref[...]
        lane = lax.broadcasted_iota(I32, (rt, EXT), 1)
        n_exp = pl.num_programs(0)
        for t in range(rows_e // rt):
            def gather(r, c):
                tok = idx_ref[base + t * rt + r]
                pltpu.make_async_copy(x_hbm.at[pl.ds(tok, 1), :], buf.at[pl.ds(r, 1), :], sem.at[1]).start()
                return c
            lax.fori_loop(0, rt, gather, 0)
            ext = xs[t * rt:(t + 1) * rt, d:]
            g = jnp.sum(jnp.where(lane == e, ext, 0.0), axis=-1, keepdims=True)
            mrow = jnp.sum(jnp.where(lane == n_exp, ext, 0.0), axis=-1, keepdims=True)
            gate_rows = jnp.zeros((rt, d), F32)
            for b in range(gates.shape[0]):
                gate_rows = jnp.where(mrow == float(b), gates[b:b + 1, :], gate_rows)
            upd = acc[t * rt:(t + 1) * rt, :] * g * gate_rows
            pltpu.make_async_copy(x_hbm.at[pl.ds(0, rt), :], buf, sem.at[1]).wait()
            buf[...] = buf[...] + upd

            def scatter(r, c):
                tok = idx_ref[base + t * rt + r]
                pltpu.make_async_copy(buf.at[pl.ds(r, 1), :], x_hbm.at[pl.ds(tok, 1), :], sem.at[2]).start()
                return c
            lax.fori_loop(0, rt, scatter, 0)
            pltpu.make_async_copy(buf, x_hbm.at[pl.ds(0, rt), :], sem.at[2]).wait()


def _moe_ffn(idx_flat, hx, x1, gates, w_gate, w_up, w_down, rows_e):
    n, d = x1.shape
    n_exp, _, f = w_gate.shape
    fcw = min(256, f)
    n_fc = f // fcw
    rt = math.gcd(rows_e, TM)
    kern = functools.partial(_moe_ffn_kernel, rows_e=rows_e, rt=rt, d=d, n_fc=n_fc)
    grid_spec = pltpu.PrefetchScalarGridSpec(
        num_scalar_prefetch=1,
        grid=(n_exp, n_fc),
        in_specs=[
            pl.BlockSpec(memory_space=pl.ANY),
            pl.BlockSpec(memory_space=pl.ANY),
            pl.BlockSpec((8, d), lambda e, c, idx: (0, 0)),
            pl.BlockSpec((None, d, fcw), lambda e, c, idx: (e, 0, c)),
            pl.BlockSpec((None, d, fcw), lambda e, c, idx: (e, 0, c)),
            pl.BlockSpec((None, fcw, d), lambda e, c, idx: (e, c, 0)),
        ],
        out_specs=pl.BlockSpec(memory_space=pl.ANY),
        scratch_shapes=[pltpu.VMEM((rows_e, d + EXT), F32), pltpu.VMEM((rows_e, d), F32),
                        pltpu.VMEM((rt, d), F32), pltpu.SemaphoreType.DMA((3,))],
    )
    return pl.pallas_call(
        kern,
        grid_spec=grid_spec,
        out_shape=jax.ShapeDtypeStruct((n, d), F32),
        input_output_aliases={2: 0},
        compiler_params=_cparams("arbitrary", "arbitrary"),
        name="moe_ffn",
    )(idx_flat, hx, x1, gates, w_gate, w_up, w_down)


def _final_norm_kernel(x_ref, w_ref, o_ref):
    o_ref[...] = _rms(x_ref[...], w_ref[...])


def _final_norm(x, w, row0, n_rows):
    d = x.shape[1]
    t0 = row0 // TM
    return pl.pallas_call(
        _final_norm_kernel,
        grid=(n_rows // TM,),
        in_specs=[pl.BlockSpec((TM, d), lambda i: (t0 + i, 0)), _full_spec((1, d))],
        out_specs=_row_spec(d),
        out_shape=jax.ShapeDtypeStruct((n_rows, d), F32),
        compiler_params=_cparams("arbitrary"),
        name="final_norm",
    )(x, w.reshape(1, d))


def _rope_tables(seq):
    rows = seq // GRID_W
    row = jnp.repeat(jnp.arange(rows), GRID_W).astype(F32)
    col = jnp.tile(jnp.arange(GRID_W), rows).astype(F32)
    nf = MLA_ROPE // 4
    inv = ROPE_BASE ** (-jnp.arange(nf, dtype=F32) / nf)
    ang = jnp.concatenate([row[:, None] * inv[None], col[:, None] * inv[None]], axis=-1)
    c, s = jnp.cos(ang), jnp.sin(ang)
    cos = jnp.concatenate([c, c, c, c], axis=-1)
    sin = jnp.concatenate([-s, s, -s, s], axis=-1)
    cos = jnp.concatenate([jnp.ones((TM, LANES), F32), cos], axis=0)
    sin = jnp.concatenate([jnp.zeros((TM, LANES), F32), sin], axis=0)
    return cos, sin


def kernel(x_prompt, x_sample, cache_ckv, cache_kpe, cache_k, cache_v, c, c_ctx, ada_w, ada_b, norm_mix, norm_ffn,
           mla_w_in, mla_q_norm, mla_w_uq, mla_kv_norm, mla_w_ukv, mla_w_o, swa_w_qkv, swa_sink, swa_w_o,
           moe_router, moe_w_gate, moe_w_up, moe_w_down, final_norm):
    assert MLA_ROPE == SWA_HEAD_DIM
    batch, seq, d = x_prompt.shape
    dec_batch, dec_seq, _ = x_sample.shape
    depth = ada_w.shape[0]
    n_p, n_s = batch * seq, dec_batch * dec_seq
    rows = _Rows(n_p, n_s, dec_seq)
    assert seq == TM and n_p % dec_seq == 0
    mla_heads = mla_w_uq.shape[-1] // (MLA_NOPE + MLA_ROPE)
    q_lora, kv_lora = mla_q_norm.shape[-1], mla_kv_norm.shape[-1]
    hq, hkv = swa_sink.shape[-1], cache_k.shape[3]
    n_exp = moe_router.shape[-1]
    cap_p, cap_s = EC_CAPACITY * n_p // n_exp, EC_CAPACITY * n_s // n_exp
    groups = ((0, n_p // CHUNK, cap_p), (n_p // CHUNK, n_s // CHUNK, cap_s))

    cv = jnp.zeros((8, d), F32).at[0].set(c_ctx).at[1:1 + dec_batch].set(c)
    mods = _adaln(cv, ada_w, ada_b)
    mods_r = mods.reshape(depth, 8, N_MOD, d).transpose(0, 2, 1, 3)
    mods_rows = mods_r.reshape(depth * N_MOD * 8, 1, d)
    cos_t, sin_t = _rope_tables(dec_seq)

    x = jnp.concatenate([x_prompt.reshape(n_p, d), x_sample.reshape(n_s, d)], axis=0)
    new_ckv, new_kpe, new_k, new_v = [], [], [], []
    for i in range(depth):
        j = i // 2
        if i % 2 == 0:
            w_in = mla_w_in[j]
            w_in_p = jnp.pad(w_in, ((0, 0), (0, q_lora + kv_lora + LANES - w_in.shape[1]))).astype(BF16)
            w_uq = mla_w_uq[j].reshape(q_lora, mla_heads, MLA_NOPE + MLA_ROPE)
            w_uq_p = jnp.pad(w_uq, ((0, 0), (0, 0), (0, 2 * LANES - MLA_NOPE - MLA_ROPE)))
            w_uq_p = w_uq_p.reshape(q_lora, mla_heads * 2 * LANES).astype(BF16)
            q, kvl = _mla_proj(rows, x, mods_rows, i, norm_mix[i], w_in_p, mla_q_norm[j], w_uq_p,
                               mla_kv_norm[j], cos_t, sin_t)
            w_ukv_b = mla_w_ukv[j].astype(BF16)
            o_p = _mla_attn(q, kvl, w_ukv_b, 0, batch, seq, mla_heads)
            o_s = _mla_attn(q, kvl, w_ukv_b, n_p, dec_batch, dec_seq, mla_heads, ctx=(cache_ckv, cache_kpe, j))
            new_ckv.append(kvl[:n_p, :kv_lora].reshape(batch, seq, kv_lora))
            new_kpe.append(kvl[:n_p, kv_lora:kv_lora + MLA_ROPE].reshape(batch, seq, MLA_ROPE))
            w_o_b = mla_w_o[j].astype(BF16)
        else:
            nq, nk = hq * SWA_HEAD_DIM, hkv * SWA_HEAD_DIM
            q, kv = _swa_proj(rows, x, mods_rows, i, norm_mix[i], swa_w_qkv[j].astype(BF16), nq, nk, cos_t, sin_t)
            o_p = _swa_ctx(q, kv, swa_sink[j], batch, seq, hq, hkv)
            ck = cache_k.reshape(cache_k.shape[:3] + (nk,))
            cvv = cache_v.reshape(cache_v.shape[:3] + (nk,))
            o_s = _swa_lat(q, kv, swa_sink[j], ck, cvv, j, n_p, dec_batch, dec_seq, hq, hkv)
            new_k.append(kv[:n_p, :nk].reshape(batch, seq, hkv, SWA_HEAD_DIM))
            new_v.append(kv[:n_p, nk:].reshape(batch, seq, hkv, SWA_HEAD_DIM))
            w_o_b = swa_w_o[j].astype(BF16)
        o = jnp.concatenate([o_p, o_s], axis=0)
        w_r = moe_router[i]
        w_router_p = jnp.pad(w_r, ((0, 0), (0, EXT - n_exp))).astype(BF16)
        x1, hx, a2 = _attn_out(rows, o, x, w_o_b, mods_rows, i, norm_ffn[i], w_router_p, w_r.T.astype(BF16))
        idx = _moe_select(a2, n_exp, groups)
        gates = mods_r[i, N_MOD - 1]
        x = _moe_ffn(idx.reshape(-1), hx, x1, gates, moe_w_gate[i], moe_w_up[i], moe_w_down[i], cap_p + cap_s)
    y_prompt = _final_norm(x, final_norm, 0, n_p).reshape(batch, seq, d)
    y_sample = _final_norm(x, final_norm, n_p, n_s).reshape(dec_batch, dec_seq, d)
    return (y_prompt, y_sample, jnp.stack(new_ckv, axis=1), jnp.stack(new_kpe, axis=1),
            jnp.stack(new_k, axis=1), jnp.stack(new_v, axis=1))
```

```python
import functools
import math

import jax
import jax.numpy as jnp
from jax import lax
from jax.experimental import pallas as pl
from jax.experimental.pallas import tpu as pltpu

F32 = jnp.float32
BF16 = jnp.bfloat16
I32 = jnp.int32
U32 = jnp.uint32
HI16 = 0xFFFF0000

GRID_W = 64
RMS_EPS = 1e-6
ROPE_BASE = 10000.0
NEG_INF = -1e30
N_MOD = 6
MLA_NOPE = 128
MLA_ROPE = 64
MLA_V = 128
SWA_HEAD_DIM = 64
SWA_WINDOW = 128
SWA_BLOCK = 128
EC_CAPACITY = 2

LANES = 128
TM = 256
MLA_TQ = 512
LOG2E = math.log2(math.e)
CHUNK = 128
EXT = 128
VMEM_LIMIT = 52 * 1024 * 1024


def _cparams(*sem):
    return pltpu.CompilerParams(dimension_semantics=sem, vmem_limit_bytes=VMEM_LIMIT)


def _rms(x, w):
    return x * lax.rsqrt(jnp.mean(x * x, axis=-1, keepdims=True) + RMS_EPS) * w


def _swap_halves(x):
    w = x.shape[-1]
    lane = lax.broadcasted_iota(I32, x.shape, x.ndim - 1)
    return jnp.where(lane % 64 < 32, pltpu.roll(x, w - 32, x.ndim - 1), pltpu.roll(x, 32, x.ndim - 1))


def _rope(x, cos, sin):
    n = x.shape[-1] // LANES
    cw = jnp.concatenate([cos] * n, axis=-1) if n > 1 else cos
    sw = jnp.concatenate([sin] * n, axis=-1) if n > 1 else sin
    return x * cw + _swap_halves(x) * sw


def _dot(a, b):
    return jnp.dot(a, b, preferred_element_type=F32)


def _dot_nt(a, b):
    return lax.dot_general(a, b, (((1,), (1,)), ((), ())), preferred_element_type=F32)


def _adaln_kernel(cv_ref, w_ref, b_ref, o_ref):
    a = cv_ref[...]
    a = a * jax.nn.sigmoid(a)
    o_ref[...] = _dot(a.astype(BF16), w_ref[...].astype(BF16)) + b_ref[...]


def _adaln(cv, ada_w, ada_b):
    depth, d, n6 = ada_w.shape
    tn = d // 2
    return pl.pallas_call(
        _adaln_kernel,
        grid=(depth, n6 // tn),
        in_specs=[
            pl.BlockSpec((8, d), lambda l, j: (0, 0)),
            pl.BlockSpec((None, d, tn), lambda l, j: (l, 0, j)),
            pl.BlockSpec((None, 1, tn), lambda l, j: (l, 0, j)),
        ],
        out_specs=pl.BlockSpec((None, 8, tn), lambda l, j: (l, 0, j)),
        out_shape=jax.ShapeDtypeStruct((depth, 8, n6), F32),
        compiler_params=_cparams("arbitrary", "arbitrary"),
        name="adaln",
    )(cv, ada_w, ada_b.reshape(depth, 1, n6))


class _Rows:
    def __init__(self, n_prompt, n_sample, dec_seq):
        self.n_prompt, self.n_sample, self.dec_seq = n_prompt, n_sample, dec_seq
        self.n = n_prompt + n_sample
        assert n_prompt % TM == 0 and dec_seq % TM == 0
        self.tiles = self.n // TM
        self.ptiles = n_prompt // TM
        self.tiles_per_seq = dec_seq // TM

    def mod_row(self, i):
        return jnp.where(i < self.ptiles, 0, 1 + (i - self.ptiles) // self.tiles_per_seq)

    def rope_block(self, i):
        return jnp.where(i < self.ptiles, 0, 1 + (i - self.ptiles) % self.tiles_per_seq)

    def mod_spec(self, d, layer, k):
        base = (layer * N_MOD + k) * 8
        return pl.BlockSpec((None, 1, d), lambda i: (base + self.mod_row(i), 0, 0))

    def rope_spec(self):
        return pl.BlockSpec((TM, LANES), lambda i: (self.rope_block(i), 0))


    def split_specs(self, width):
        return [pl.BlockSpec((TM, width), lambda i: (jnp.minimum(i, self.ptiles - 1), 0)),
                pl.BlockSpec((TM, width), lambda i: (jnp.maximum(i - self.ptiles, 0), 0))]


def _pick_rows(p_ref, s_ref, ptiles):
    return jnp.where(pl.program_id(0) < ptiles, p_ref[...], s_ref[...])


def _row_spec(width):
    return pl.BlockSpec((TM, width), lambda i: (i, 0))


def _full_spec(shape):
    nd = len(shape)
    return pl.BlockSpec(shape, lambda i: (0,) * nd)


def _mla_proj_kernel(*refs, q_lora, kv_lora, heads, split_ptiles):
    if split_ptiles:
        x = _pick_rows(refs[0], refs[1], split_ptiles)
        refs = refs[2:]
    else:
        x = refs[0][...]
        refs = refs[1:]
    nw_ref, sh_ref, sc_ref, win_ref, qn_ref, wuq_ref, kvn_ref, cos_ref, sin_ref, q_ref, kvl_ref = refs
    h = _rms(x, nw_ref[...]) * (1.0 + sc_ref[...]) + sh_ref[...]
    a = _dot(h.astype(BF16), win_ref[...])
    cq = a[:, :q_lora]
    ckv = a[:, q_lora:q_lora + kv_lora]
    kpe = a[:, q_lora + kv_lora:q_lora + kv_lora + LANES]
    cos, sin = cos_ref[...], sin_ref[...]
    q = _dot(_rms(cq, qn_ref[...]).astype(BF16), wuq_ref[...])
    for hh in range(heads):
        lo = hh * 2 * LANES
        q_ref[:, lo:lo + LANES] = q[:, lo:lo + LANES].astype(BF16)
        q_ref[:, lo + LANES:lo + 2 * LANES] = _rope(q[:, lo + LANES:lo + 2 * LANES], cos, sin).astype(BF16)
    kvl_ref[:, :kv_lora] = _rms(ckv, kvn_ref[...])
    kvl_ref[:, kv_lora:] = _rope(kpe, cos, sin)


def _mla_proj(rows, xs, mods, layer, norm_w, w_in_p, q_norm, w_uq_p, kv_norm, cos_t, sin_t):
    d = xs[0].shape[1]
    q_lora, kv_lora = q_norm.shape[-1], kv_norm.shape[-1]
    heads = w_uq_p.shape[1] // (2 * LANES)
    split = len(xs) == 2
    kern = functools.partial(_mla_proj_kernel, q_lora=q_lora, kv_lora=kv_lora, heads=heads,
                             split_ptiles=rows.ptiles if split else 0)
    return pl.pallas_call(
        kern,
        grid=(rows.tiles,),
        in_specs=(rows.split_specs(d) if split else [_row_spec(d)]) + [
            _full_spec((1, d)), rows.mod_spec(d, layer, 0), rows.mod_spec(d, layer, 1),
            _full_spec(w_in_p.shape), _full_spec((1, q_lora)), _full_spec(w_uq_p.shape),
            _full_spec((1, kv_lora)), rows.rope_spec(), rows.rope_spec(),
        ],
        out_specs=[_row_spec(heads * 2 * LANES), _row_spec(kv_lora + LANES)],
        out_shape=[jax.ShapeDtypeStruct((rows.n, heads * 2 * LANES), BF16),
                   jax.ShapeDtypeStruct((rows.n, kv_lora + LANES), F32)],
        compiler_params=_cparams("arbitrary"),
        name="mla_proj",
    )(*xs, norm_w.reshape(1, d), mods, mods, w_in_p, q_norm.reshape(1, -1), w_uq_p, kv_norm.reshape(1, -1),
      cos_t, sin_t)


def _mla_attn_kernel(*refs, n_ctx, kv_lora, scale):
    if n_ctx:
        q_ref, cckv_ref, ckpe_ref, kvl_ref, wukv_ref, o_ref, k_s, v_s = refs
    else:
        q_ref, kvl_ref, wukv_ref, o_ref, k_s, v_s = refs

    @pl.when(pl.program_id(2) == 0)
    def _():
        w = wukv_ref[...]
        lat = kvl_ref[...]
        kv = _dot(lat[:, :kv_lora].astype(BF16), w)
        k_s[n_ctx:, :MLA_NOPE] = kv[:, :MLA_NOPE].astype(BF16)
        k_s[n_ctx:, MLA_NOPE:] = lat[:, kv_lora:].astype(BF16)
        v_s[n_ctx:, :] = kv[:, MLA_NOPE:].astype(BF16)
        if n_ctx:
            kvc = _dot(cckv_ref[...].astype(BF16), w)
            k_s[:n_ctx, :MLA_NOPE] = kvc[:, :MLA_NOPE].astype(BF16)
            k_s[:n_ctx, MLA_NOPE:MLA_NOPE + MLA_ROPE] = ckpe_ref[...].astype(BF16)
            k_s[:n_ctx, MLA_NOPE + MLA_ROPE:] = jnp.zeros((n_ctx, LANES - MLA_ROPE), BF16)
            v_s[:n_ctx, :] = kvc[:, MLA_NOPE:].astype(BF16)

    s = _dot_nt(q_ref[...], k_s[...]) * (scale * LOG2E)
    m = jnp.max(s, axis=-1, keepdims=True)
    p = jnp.exp2(s - m)
    l = jnp.sum(p, axis=-1, keepdims=True)
    o_ref[...] = (_dot(p.astype(BF16), v_s[...]) / l).astype(BF16)


def _mla_attn(q, kvl, w_ukv_b, row0, nb, seq, heads, ctx=None):
    kv_lora = kvl.shape[1] - LANES
    tq = min(MLA_TQ, seq)
    nq = seq // tq
    assert row0 % seq == 0
    seq0, tile0 = row0 // seq, row0 // tq
    n_ctx = 0 if ctx is None else ctx[0].shape[2]
    lk = n_ctx + seq
    scale = 1.0 / math.sqrt(MLA_NOPE + MLA_ROPE)
    in_specs = [pl.BlockSpec((tq, 2 * LANES), lambda b, h, i: (tile0 + b * nq + i, h))]
    args = [q]
    if ctx is not None:
        cckv, ckpe, j = ctx
        in_specs += [pl.BlockSpec((None, None, n_ctx, kv_lora), lambda b, h, i: (b, j, 0, 0)),
                     pl.BlockSpec((None, None, n_ctx, MLA_ROPE), lambda b, h, i: (b, j, 0, 0))]
        args += [cckv, ckpe]
    in_specs += [pl.BlockSpec((seq, kv_lora + LANES), lambda b, h, i: (seq0 + b, 0)),
                 pl.BlockSpec((kv_lora, MLA_NOPE + MLA_V), lambda b, h, i: (0, h))]
    args += [kvl, w_ukv_b]
    kern = functools.partial(_mla_attn_kernel, n_ctx=n_ctx, kv_lora=kv_lora, scale=scale)
    return pl.pallas_call(
        kern,
        grid=(nb, heads, nq),
        in_specs=in_specs,
        out_specs=pl.BlockSpec((tq, MLA_V), lambda b, h, i: (b * nq + i, h)),
        out_shape=jax.ShapeDtypeStruct((nb * seq, heads * MLA_V), BF16),
        scratch_shapes=[pltpu.VMEM((lk, 2 * LANES), BF16), pltpu.VMEM((lk, MLA_V), BF16)],
        compiler_params=_cparams("arbitrary", "arbitrary", "arbitrary"),
        name="mla_attn_ctx" if ctx is None else "mla_attn_lat",
    )(*args)


def _swa_proj_kernel(x_ref, nw_ref, sh_ref, sc_ref, w_ref, cos_ref, sin_ref, q_ref, kv_ref, *, nq, nk):
    x = x_ref[...]
    h = _rms(x, nw_ref[...]) * (1.0 + sc_ref[...]) + sh_ref[...]
    a = _dot(h.astype(BF16), w_ref[...])
    qk = _rope(a[:, :nq + nk], cos_ref[...], sin_ref[...])
    q_ref[...] = qk[:, :nq].astype(BF16)
    kv_ref[:, :nk] = qk[:, nq:]
    kv_ref[:, nk:] = a[:, nq + nk:]


def _swa_proj(rows, x, mods, layer, norm_w, w_qkv_b, nq, nk, cos_t, sin_t):
    d = x.shape[1]
    kern = functools.partial(_swa_proj_kernel, nq=nq, nk=nk)
    return pl.pallas_call(
        kern,
        grid=(rows.tiles,),
        in_specs=[_row_spec(d), _full_spec((1, d)), rows.mod_spec(d, layer, 0), rows.mod_spec(d, layer, 1),
                  _full_spec(w_qkv_b.shape), rows.rope_spec(), rows.rope_spec()],
        out_specs=[_row_spec(nq), _row_spec(2 * nk)],
        out_shape=[jax.ShapeDtypeStruct((rows.n, nq), BF16), jax.ShapeDtypeStruct((rows.n, 2 * nk), F32)],
        compiler_params=_cparams("arbitrary"),
        name="swa_proj",
    )(x, norm_w.reshape(1, d), mods, mods, w_qkv_b, cos_t, sin_t)


def _sink_softmax_pv(s_parts, v_parts, sink):
    m = jnp.maximum(functools.reduce(jnp.maximum, [jnp.max(s, axis=-1, keepdims=True) for s in s_parts]), sink)
    l = jnp.exp2(sink - m)
    o = None
    for s, v in zip(s_parts, v_parts):
        p = jnp.exp2(s - m)
        l = l + jnp.sum(p, axis=-1, keepdims=True)
        pv = _dot(p.astype(BF16), v)
        o = pv if o is None else o + pv
    return o / l


def _stack_heads(q, g, group, dh):
    return jnp.concatenate([q[:, (g * group + t) * dh:(g * group + t + 1) * dh] for t in range(group)], axis=0)


def _sink_column(sink_ref, g, group, rows):
    return jnp.concatenate([jnp.full((rows, 1), sink_ref[g * group + t] * LOG2E, F32) for t in range(group)], axis=0)


def _swa_ctx_kernel(sink_ref, q_ref, kv_ref, o_ref, *, hq, hkv):
    dh = SWA_HEAD_DIM
    group = hq // hkv
    c = LOG2E / math.sqrt(dh)
    rows = q_ref.shape[0]
    q = q_ref[...]
    kv = kv_ref[...].astype(BF16)
    for g in range(hkv):
        k = kv[:, g * dh:(g + 1) * dh]
        v = kv[:, (hkv + g) * dh:(hkv + g + 1) * dh]
        s = _dot_nt(_stack_heads(q, g, group, dh), k) * c
        o = _sink_softmax_pv([s], [v], _sink_column(sink_ref, g, group, rows)).astype(BF16)
        for t in range(group):
            hh = g * group + t
            o_ref[:, hh * dh:(hh + 1) * dh] = o[t * rows:(t + 1) * rows, :]


def _swa_ctx(q, kv, sink, nb, seq, hq, hkv):
    kern = functools.partial(_swa_ctx_kernel, hq=hq, hkv=hkv)
    return pl.pallas_call(
        kern,
        grid=(nb,),
        in_specs=[pl.BlockSpec(memory_space=pltpu.SMEM),
                  pl.BlockSpec((seq, hq * SWA_HEAD_DIM), lambda b: (b, 0)),
                  pl.BlockSpec((seq, 2 * hkv * SWA_HEAD_DIM), lambda b: (b, 0))],
        out_specs=pl.BlockSpec((seq, hq * SWA_HEAD_DIM), lambda b: (b, 0)),
        out_shape=jax.ShapeDtypeStruct((nb * seq, hq * SWA_HEAD_DIM), BF16),
        compiler_params=_cparams("arbitrary"),
        name="swa_attn_ctx",
    )(sink, q, kv)


def _swa_lat_kernel(sink_ref, q_ref, kc_ref, vc_ref, kv0_ref, kv1_ref, kv2_ref, o_ref, *, hq, hkv, seq):
    dh = SWA_HEAD_DIM
    group = hq // hkv
    c = LOG2E / math.sqrt(dh)
    blk = SWA_BLOCK
    i = pl.program_id(1)
    qpos = i * blk + lax.broadcasted_iota(I32, (group * blk, 3 * blk), 0) % blk
    kpos = (i - 1) * blk + lax.broadcasted_iota(I32, (group * blk, 3 * blk), 1)
    band = (jnp.abs(qpos - kpos) <= SWA_WINDOW) & (kpos >= 0) & (kpos < seq)
    q = q_ref[...]
    kc = kc_ref[...].astype(BF16)
    vc = vc_ref[...].astype(BF16)
    kvb = jnp.concatenate([kv0_ref[...], kv1_ref[...], kv2_ref[...]], axis=0).astype(BF16)
    for g in range(hkv):
        k_c = kc[:, g * dh:(g + 1) * dh]
        v_c = vc[:, g * dh:(g + 1) * dh]
        k_b = kvb[:, g * dh:(g + 1) * dh]
        v_b = kvb[:, (hkv + g) * dh:(hkv + g + 1) * dh]
        qg = _stack_heads(q, g, group, dh)
        s_c = _dot_nt(qg, k_c) * c
        s_b = jnp.where(band, _dot_nt(qg, k_b) * c, NEG_INF)
        o = _sink_softmax_pv([s_c, s_b], [v_c, v_b], _sink_column(sink_ref, g, group, blk)).astype(BF16)
        for t in range(group):
            hh = g * group + t
            o_ref[:, hh * dh:(hh + 1) * dh] = o[t * blk:(t + 1) * blk, :]


def _swa_lat(q, kv, sink, cache_k, cache_v, j, row0, nb, seq, hq, hkv):
    blk = SWA_BLOCK
    nblk = seq // blk
    blk0 = row0 // blk
    n_ctx = cache_k.shape[2]
    wq, wkv = hq * SWA_HEAD_DIM, hkv * SWA_HEAD_DIM
    kern = functools.partial(_swa_lat_kernel, hq=hq, hkv=hkv, seq=seq)

    def band_spec(off):
        return pl.BlockSpec((blk, 2 * wkv), lambda b, i: (blk0 + b * nblk + jnp.clip(i + off, 0, nblk - 1), 0))

    return pl.pallas_call(
        kern,
        grid=(nb, nblk),
        in_specs=[pl.BlockSpec(memory_space=pltpu.SMEM),
                  pl.BlockSpec((blk, wq), lambda b, i: (blk0 + b * nblk + i, 0)),
                  pl.BlockSpec((None, None, n_ctx, wkv), lambda b, i: (b, j, 0, 0)),
                  pl.BlockSpec((None, None, n_ctx, wkv), lambda b, i: (b, j, 0, 0)),
                  band_spec(-1), band_spec(0), band_spec(1)],
        out_specs=pl.BlockSpec((blk, wq), lambda b, i: (b * nblk + i, 0)),
        out_shape=jax.ShapeDtypeStruct((nb * seq, wq), BF16),
        compiler_params=_cparams("arbitrary", "arbitrary"),
        name="swa_attn_lat",
    )(sink, q, cache_k, cache_v, kv, kv, kv)


def _attn_out_kernel(*refs, n_exp, ptiles, tiles_per_seq, split_x):
    o = _pick_rows(refs[0], refs[1], ptiles)
    if split_x:
        x = _pick_rows(refs[2], refs[3], ptiles)
        refs = refs[4:]
    else:
        x = refs[2][...]
        refs = refs[3:]
    wo_ref, gate_ref, nw_ref, sh_ref, sc_ref, wr_ref, wrt_ref, x1_ref, hx_ref, a2_ref = refs
    half = x.shape[1] // 2
    x1 = x + gate_ref[...] * _dot(o, wo_ref[...])
    x1_ref[...] = x1
    h = _rms(x1, nw_ref[...]) * (1.0 + sc_ref[...]) + sh_ref[...]
    hb = h.astype(BF16)
    bits = lax.bitcast_convert_type(hb.astype(F32), U32)
    hx_ref[:, :half] = (bits[:, half:] & U32(HI16)) | (bits[:, :half] >> U32(16))
    lane = lax.broadcasted_iota(I32, (TM, EXT), 1)
    logits = jnp.where(lane < n_exp, _dot(hb, wr_ref[...]), NEG_INF)
    e = jnp.exp(logits - jnp.max(logits, axis=-1, keepdims=True))
    aff = e / jnp.sum(e, axis=-1, keepdims=True)
    i = pl.program_id(0)
    mod_row = jnp.where(i < ptiles, 0, 1 + (i - ptiles) // tiles_per_seq).astype(F32)
    hx_ref[:, half:] = lax.bitcast_convert_type(jnp.where(lane == n_exp, mod_row, aff), U32)
    lt = _dot_nt(wrt_ref[...], hb)
    et = jnp.exp(lt - jnp.max(lt, axis=0, keepdims=True))
    at = et / jnp.sum(et, axis=0, keepdims=True)
    a2_ref[...] = jnp.concatenate([at[:, c * CHUNK:(c + 1) * CHUNK] for c in range(TM // CHUNK)], axis=0)


def _attn_out(rows, o_p, o_s, xs, w_o_b, mods, layer, norm_w, w_router_p, w_router_t):
    d = xs[0].shape[1]
    n_exp = w_router_t.shape[0]
    split_x = len(xs) == 2
    kern = functools.partial(_attn_out_kernel, n_exp=n_exp, ptiles=rows.ptiles, tiles_per_seq=rows.tiles_per_seq,
                             split_x=split_x)
    cpt = TM // CHUNK
    return pl.pallas_call(
        kern,
        grid=(rows.tiles,),
        in_specs=rows.split_specs(o_p.shape[1]) + (rows.split_specs(d) if split_x else [_row_spec(d)]) + [
            _full_spec(w_o_b.shape), rows.mod_spec(d, layer, 2),
            _full_spec((1, d)), rows.mod_spec(d, layer, 3), rows.mod_spec(d, layer, 4),
            _full_spec(w_router_p.shape), _full_spec(w_router_t.shape)],
        out_specs=[_row_spec(d), _row_spec(d // 2 + EXT), pl.BlockSpec((cpt * n_exp, CHUNK), lambda i: (i, 0))],
        out_shape=[jax.ShapeDtypeStruct((rows.n, d), F32), jax.ShapeDtypeStruct((rows.n, d // 2 + EXT), U32),
                   jax.ShapeDtypeStruct((rows.n // CHUNK * n_exp, CHUNK), F32)],
        compiler_params=_cparams("arbitrary"),
        name="attn_out",
    )(o_p, o_s, *xs, w_o_b, mods, norm_w.reshape(1, d), mods, mods, w_router_p, w_router_t)


def _select_group(a3, cap):
    nj, n_exp, _ = a3.shape
    bits = lax.bitcast_convert_type(a3, I32)
    capf = jnp.float32(cap)

    def count(pred):
        c = jnp.sum(pred.astype(F32), axis=0)
        return jnp.sum(c, axis=-1, keepdims=True)[None]

    def thr_step(k, t):
        cand = t | lax.shift_left(jnp.int32(1), 30 - k)
        return jnp.where(count(bits >= cand) >= capf, cand, t)

    thr = lax.fori_loop(0, 31, thr_step, jnp.zeros((1, n_exp, 1), I32))
    gt = bits > thr
    eq = bits == thr
    need = capf - count(gt)
    tok = (lax.broadcasted_iota(I32, a3.shape, 0) * CHUNK + lax.broadcasted_iota(I32, a3.shape, 2))
    nbits = max(1, (nj * CHUNK).bit_length())

    def tie_step(k, x):
        cand = x | lax.shift_left(jnp.int32(1), nbits - 1 - k)
        return jnp.where(count(eq & (tok < cand)) <= need, cand, x)

    bound = lax.fori_loop(0, nbits, tie_step, jnp.zeros((1, n_exp, 1), I32))
    return (gt | (eq & (tok < bound))).astype(F32)


def _compact(mask_e, cap, tri, lstrict):
    cl = _dot(mask_e.astype(BF16), tri)
    cnt = cl[:, CHUNK - 1:CHUNK]
    off = _dot(lstrict, jnp.broadcast_to(cnt, (LANES, LANES)).astype(BF16))[:, :1]
    ends = off + cnt
    r = lax.broadcasted_iota(I32, (1, cap), 1).astype(F32)
    jstar = jnp.sum((ends <= r).astype(F32), axis=0, keepdims=True)
    onehot = lax.broadcasted_iota(I32, (LANES, cap), 0).astype(F32) == jstar
    offsel = jnp.sum(jnp.where(onehot, off, 0.0), axis=0, keepdims=True)
    g = _dot(cl.T.astype(BF16), onehot.astype(BF16))
    local = jnp.sum((g <= r - offsel).astype(F32), axis=0, keepdims=True)
    return (jstar * CHUNK + local).astype(I32)


def _moe_select_kernel(a2_ref, tri_ref, ls_ref, idx_ref, m_s, *, n_exp, groups):
    tri = tri_ref[...]
    lstrict = ls_ref[...]
    col = 0
    for (chunk0, nj, cap) in groups:
        m_s[...] = jnp.zeros(m_s.shape, F32)
        a3 = a2_ref[chunk0 * n_exp:(chunk0 + nj) * n_exp, :].reshape(nj, n_exp, CHUNK)
        m_s[:nj * n_exp, :] = _select_group(a3, cap).reshape(nj * n_exp, CHUNK)
        for e in range(n_exp):
            mask_e = m_s[pl.ds(e, LANES, stride=n_exp), :]
            idx_ref[e:e + 1, col:col + cap] = _compact(mask_e, cap, tri, lstrict) + chunk0 * CHUNK
        col += cap


def _moe_select(a2, n_exp, groups):
    total = sum(g[2] for g in groups)
    assert all(g[1] <= LANES and g[2] % LANES == 0 for g in groups)
    ii = jnp.arange(LANES)
    tri = (ii[:, None] <= ii[None, :]).astype(BF16)
    lstrict = (ii[None, :] < ii[:, None]).astype(BF16)
    kern = functools.partial(_moe_select_kernel, n_exp=n_exp, groups=groups)
    return pl.pallas_call(
        kern,
        grid=(1,),
        in_specs=[_full_spec(a2.shape), _full_spec((LANES, LANES)), _full_spec((LANES, LANES))],
        out_specs=_full_spec((n_exp, total)),
        out_shape=jax.ShapeDtypeStruct((n_exp, total), I32),
        scratch_shapes=[pltpu.VMEM((LANES * n_exp, CHUNK), F32)],
        compiler_params=_cparams("arbitrary"),
        name="moe_select",
    )(a2, tri, lstrict)


def _moe_ffn_kernel(idx_ref, hx_hbm, x_in_hbm, gates_ref, wg_ref, wu_ref, wd_ref, x_hbm,
                    xs, acc, buf, sem_h, sem_g, sem_s, *, rows_e, rt, d, n_fc):
    del x_in_hbm
    e = pl.program_id(0)
    fc = pl.program_id(1)
    n_exp = pl.num_programs(0)
    half = d // 2
    n_t = rows_e // rt
    slot = e % 2
    unroll = 8

    def h_gather(ex, sl):
        def body(r, c):
            tok = idx_ref[ex * rows_e + r]
            pltpu.make_async_copy(hx_hbm.at[pl.ds(tok, 1), :], xs.at[sl, pl.ds(r, 1), :], sem_h.at[sl]).start()
            return c
        lax.fori_loop(0, rows_e, body, 0, unroll=unroll)

    def x_rows(t, b, to_vmem):
        def body(r, c):
            tok = idx_ref[e * rows_e + t * rt + r]
            hbm, vmem = x_hbm.at[pl.ds(tok, 1), :], buf.at[b, pl.ds(r, 1), :]
            if to_vmem:
                pltpu.make_async_copy(hbm, vmem, sem_g.at[b]).start()
            else:
                pltpu.make_async_copy(vmem, hbm, sem_s.at[b]).start()
            return c
        lax.fori_loop(0, rt, body, 0, unroll=unroll)

    def x_wait(b, to_vmem):
        if to_vmem:
            pltpu.make_async_copy(x_hbm.at[pl.ds(0, rt), :], buf.at[b], sem_g.at[b]).wait()
        else:
            pltpu.make_async_copy(buf.at[b], x_hbm.at[pl.ds(0, rt), :], sem_s.at[b]).wait()

    @pl.when((fc == 0) & (e == 0))
    def _():
        h_gather(0, 0)

    @pl.when(fc == 0)
    def _():
        pltpu.make_async_copy(hx_hbm.at[pl.ds(0, rows_e), :], xs.at[slot], sem_h.at[slot]).wait()

    @pl.when((fc == 0) & (e + 1 < n_exp))
    def _():
        h_gather(e + 1, 1 - slot)

    tail = sorted({t % 2 for t in range(max(n_t - 2, 0), n_t)})

    @pl.when((fc == n_fc - 1) & (e > 0))
    def _():
        for b in tail:
            x_wait(b, False)

    @pl.when(fc == n_fc - 1)
    def _():
        x_rows(0, 0, True)

    wg = wg_ref[...].astype(BF16)
    wu = wu_ref[...].astype(BF16)
    wd = wd_ref[...].astype(BF16)
    for t in range(n_t):
        w = xs[slot, t * rt:(t + 1) * rt, :half]
        x_lo = lax.bitcast_convert_type(w << U32(16), F32).astype(BF16)
        x_hi = lax.bitcast_convert_type(w & U32(HI16), F32).astype(BF16)
        gate = _dot(x_lo, wg[:half]) + _dot(x_hi, wg[half:])
        up = _dot(x_lo, wu[:half]) + _dot(x_hi, wu[half:])
        part = _dot((gate * jax.nn.sigmoid(gate) * up).astype(BF16), wd)

        @pl.when(fc == 0)
        def _():
            acc[t * rt:(t + 1) * rt, :] = part

        @pl.when(fc > 0)
        def _():
            acc[t * rt:(t + 1) * rt, :] += part

    @pl.when(fc == n_fc - 1)
    def _():
        gates = gates_ref[...]
        lane = lax.broadcasted_iota(I32, (rt, EXT), 1)
        for t in range(n_t):
            b = t % 2
            if t + 1 < n_t:
                if t >= 1:
                    x_wait(1 - b, False)
                x_rows(t + 1, 1 - b, True)
            ext = lax.bitcast_convert_type(xs[slot, t * rt:(t + 1) * rt, half:], F32)
            g = jnp.sum(jnp.where(lane == e, ext, 0.0), axis=-1, keepdims=True)
            mrow = jnp.sum(jnp.where(lane == n_exp, ext, 0.0), axis=-1, keepdims=True)
            gate_rows = jnp.zeros((rt, d), F32)
            for m in range(gates.shape[0]):
                gate_rows = jnp.where(mrow == float(m), gates[m:m + 1, :], gate_rows)
            upd = acc[t * rt:(t + 1) * rt, :] * g * gate_rows
            x_wait(b, True)
            buf[b] = buf[b] + upd
            x_rows(t, b, False)

    @pl.when((fc == n_fc - 1) & (e == n_exp - 1))
    def _():
        for b in tail:
            x_wait(b, False)


def _moe_ffn(idx_flat, hx, x1, gates, w_gate, w_up, w_down, layer, rows_e):
    n, d = x1.shape
    _, n_exp, _, f = w_gate.shape
    fcw = min(256, f)
    n_fc = f // fcw
    rt = math.gcd(rows_e // 2, TM)
    kern = functools.partial(_moe_ffn_kernel, rows_e=rows_e, rt=rt, d=d, n_fc=n_fc)
    grid_spec = pltpu.PrefetchScalarGridSpec(
        num_scalar_prefetch=1,
        grid=(n_exp, n_fc),
        in_specs=[
            pl.BlockSpec(memory_space=pl.ANY),
            pl.BlockSpec(memory_space=pl.ANY),
            pl.BlockSpec((8, d), lambda e, c, idx: (0, 0)),
            pl.BlockSpec((None, None, d, fcw), lambda e, c, idx: (layer, e, 0, c)),
            pl.BlockSpec((None, None, d, fcw), lambda e, c, idx: (layer, e, 0, c)),
            pl.BlockSpec((None, None, fcw, d), lambda e, c, idx: (layer, e, c, 0)),
        ],
        out_specs=pl.BlockSpec(memory_space=pl.ANY),
        scratch_shapes=[pltpu.VMEM((2, rows_e, d // 2 + EXT), U32), pltpu.VMEM((rows_e, d), F32),
                        pltpu.VMEM((2, rt, d), F32), pltpu.SemaphoreType.DMA((2,)),
                        pltpu.SemaphoreType.DMA((2,)), pltpu.SemaphoreType.DMA((2,))],
    )
    return pl.pallas_call(
        kern,
        grid_spec=grid_spec,
        out_shape=jax.ShapeDtypeStruct((n, d), F32),
        input_output_aliases={2: 0},
        compiler_params=_cparams("arbitrary", "arbitrary"),
        name="moe_ffn",
    )(idx_flat, hx, x1, gates, w_gate, w_up, w_down)


def _final_norm_kernel(x_ref, w_ref, o_ref):
    o_ref[...] = _rms(x_ref[...], w_ref[...])


def _final_norm(x, w, row0, n_rows):
    d = x.shape[1]
    t0 = row0 // TM
    return pl.pallas_call(
        _final_norm_kernel,
        grid=(n_rows // TM,),
        in_specs=[pl.BlockSpec((TM, d), lambda i: (t0 + i, 0)), _full_spec((1, d))],
        out_specs=_row_spec(d),
        out_shape=jax.ShapeDtypeStruct((n_rows, d), F32),
        compiler_params=_cparams("arbitrary"),
        name="final_norm",
    )(x, w.reshape(1, d))


def _rope_tables(seq):
    rows = seq // GRID_W
    row = jnp.repeat(jnp.arange(rows), GRID_W).astype(F32)
    col = jnp.tile(jnp.arange(GRID_W), rows).astype(F32)
    nf = MLA_ROPE // 4
    inv = ROPE_BASE ** (-jnp.arange(nf, dtype=F32) / nf)
    ang = jnp.concatenate([row[:, None] * inv[None], col[:, None] * inv[None]], axis=-1)
    c, s = jnp.cos(ang), jnp.sin(ang)
    cos = jnp.concatenate([c, c, c, c], axis=-1)
    sin = jnp.concatenate([-s, s, -s, s], axis=-1)
    cos = jnp.concatenate([jnp.ones((TM, LANES), F32), cos], axis=0)
    sin = jnp.concatenate([jnp.zeros((TM, LANES), F32), sin], axis=0)
    return cos, sin


def kernel(x_prompt, x_sample, cache_ckv, cache_kpe, cache_k, cache_v, c, c_ctx, ada_w, ada_b, norm_mix, norm_ffn,
           mla_w_in, mla_q_norm, mla_w_uq, mla_kv_norm, mla_w_ukv, mla_w_o, swa_w_qkv, swa_sink, swa_w_o,
           moe_router, moe_w_gate, moe_w_up, moe_w_down, final_norm):
    assert MLA_ROPE == SWA_HEAD_DIM
    batch, seq, d = x_prompt.shape
    dec_batch, dec_seq, _ = x_sample.shape
    depth = ada_w.shape[0]
    n_p, n_s = batch * seq, dec_batch * dec_seq
    rows = _Rows(n_p, n_s, dec_seq)
    assert seq == TM and n_p % dec_seq == 0
    mla_heads = mla_w_uq.shape[-1] // (MLA_NOPE + MLA_ROPE)
    q_lora, kv_lora = mla_q_norm.shape[-1], mla_kv_norm.shape[-1]
    hq, hkv = swa_sink.shape[-1], cache_k.shape[3]
    n_exp = moe_router.shape[-1]
    cap_p, cap_s = EC_CAPACITY * n_p // n_exp, EC_CAPACITY * n_s // n_exp
    groups = ((0, n_p // CHUNK, cap_p), (n_p // CHUNK, n_s // CHUNK, cap_s))

    cv = jnp.zeros((8, d), F32).at[0].set(c_ctx).at[1:1 + dec_batch].set(c)
    mods = _adaln(cv, ada_w, ada_b)
    mods_r = mods.reshape(depth, 8, N_MOD, d).transpose(0, 2, 1, 3)
    mods_rows = mods_r.reshape(depth * N_MOD * 8, 1, d)
    cos_t, sin_t = _rope_tables(dec_seq)

    xs = (x_prompt.reshape(n_p, d), x_sample.reshape(n_s, d))
    new_ckv, new_kpe, new_k, new_v = [], [], [], []
    for i in range(depth):
        j = i // 2
        if i % 2 == 0:
            w_in = mla_w_in[j]
            w_in_p = jnp.pad(w_in, ((0, 0), (0, q_lora + kv_lora + LANES - w_in.shape[1]))).astype(BF16)
            w_uq = mla_w_uq[j].reshape(q_lora, mla_heads, MLA_NOPE + MLA_ROPE)
            w_uq_p = jnp.pad(w_uq, ((0, 0), (0, 0), (0, 2 * LANES - MLA_NOPE - MLA_ROPE)))
            w_uq_p = w_uq_p.reshape(q_lora, mla_heads * 2 * LANES).astype(BF16)
            q, kvl = _mla_proj(rows, xs, mods_rows, i, norm_mix[i], w_in_p, mla_q_norm[j], w_uq_p,
                               mla_kv_norm[j], cos_t, sin_t)
            w_ukv_b = mla_w_ukv[j].astype(BF16)
            o_p = _mla_attn(q, kvl, w_ukv_b, 0, batch, seq, mla_heads)
            o_s = _mla_attn(q, kvl, w_ukv_b, n_p, dec_batch, dec_seq, mla_heads, ctx=(cache_ckv, cache_kpe, j))
            new_ckv.append(kvl[:n_p, :kv_lora].reshape(batch, seq, kv_lora))
            new_kpe.append(kvl[:n_p, kv_lora:kv_lora + MLA_ROPE].reshape(batch, seq, MLA_ROPE))
            w_o_b = mla_w_o[j].astype(BF16)
        else:
            nq, nk = hq * SWA_HEAD_DIM, hkv * SWA_HEAD_DIM
            q, kv = _swa_proj(rows, xs[0], mods_rows, i, norm_mix[i], swa_w_qkv[j].astype(BF16), nq, nk,
                              cos_t, sin_t)
            o_p = _swa_ctx(q, kv, swa_sink[j], batch, seq, hq, hkv)
            ck = cache_k.reshape(cache_k.shape[:3] + (nk,))
            cvv = cache_v.reshape(cache_v.shape[:3] + (nk,))
            o_s = _swa_lat(q, kv, swa_sink[j], ck, cvv, j, n_p, dec_batch, dec_seq, hq, hkv)
            new_k.append(kv[:n_p, :nk].reshape(batch, seq, hkv, SWA_HEAD_DIM))
            new_v.append(kv[:n_p, nk:].reshape(batch, seq, hkv, SWA_HEAD_DIM))
            w_o_b = swa_w_o[j].astype(BF16)
        w_r = moe_router[i]
        w_router_p = jnp.pad(w_r, ((0, 0), (0, EXT - n_exp))).astype(BF16)
        x1, hx, a2 = _attn_out(rows, o_p, o_s, xs, w_o_b, mods_rows, i, norm_ffn[i], w_router_p,
                               w_r.T.astype(BF16))
        idx = _moe_select(a2, n_exp, groups)
        gates = mods_r[i, N_MOD - 1]
        xs = (_moe_ffn(idx.reshape(-1), hx, x1, gates, moe_w_gate, moe_w_up, moe_w_down, i, cap_p + cap_s),)
    x = xs[0]
    y_prompt = _final_norm(x, final_norm, 0, n_p).reshape(batch, seq, d)
    y_sample = _final_norm(x, final_norm, n_p, n_s).reshape(dec_batch, dec_seq, d)
    return (y_prompt, y_sample, jnp.stack(new_ckv, axis=1), jnp.stack(new_kpe, axis=1),
            jnp.stack(new_k, axis=1), jnp.stack(new_v, axis=1))
```

```python
import functools
import math

import jax
import jax.numpy as jnp
from jax import lax
from jax.experimental import pallas as pl
from jax.experimental.pallas import tpu as pltpu

F32 = jnp.float32
BF16 = jnp.bfloat16
I32 = jnp.int32
U32 = jnp.uint32
HI16 = 0xFFFF0000

GRID_W = 64
RMS_EPS = 1e-6
ROPE_BASE = 10000.0
NEG_INF = -1e30
N_MOD = 6
MLA_NOPE = 128
MLA_ROPE = 64
MLA_V = 128
SWA_HEAD_DIM = 64
SWA_WINDOW = 128
SWA_BLOCK = 128
EC_CAPACITY = 2

LANES = 128
SUBLANES = 8
TM = 256
MLA_TQ = 256
MLA_HEADS_PER_STEP = 4
LOG2E = math.log2(math.e)
CHUNK = 128
EXT = 128
VMEM_LIMIT = 52 * 1024 * 1024


def _cparams(*sem):
    return pltpu.CompilerParams(dimension_semantics=sem, vmem_limit_bytes=VMEM_LIMIT)


def _rms(x, w):
    return x * lax.rsqrt(jnp.mean(x * x, axis=-1, keepdims=True) + RMS_EPS) * w


def _swap_halves(x):
    w = x.shape[-1]
    lane = lax.broadcasted_iota(I32, x.shape, x.ndim - 1)
    return jnp.where(lane % 64 < 32, pltpu.roll(x, w - 32, x.ndim - 1), pltpu.roll(x, 32, x.ndim - 1))


def _rope(x, cos, sin):
    n = x.shape[-1] // LANES
    cw = jnp.concatenate([cos] * n, axis=-1) if n > 1 else cos
    sw = jnp.concatenate([sin] * n, axis=-1) if n > 1 else sin
    return x * cw + _swap_halves(x) * sw


def _dot(a, b):
    return jnp.dot(a, b, preferred_element_type=F32)


def _dot_nt(a, b):
    return lax.dot_general(a, b, (((1,), (1,)), ((), ())), preferred_element_type=F32)


def _adaln_kernel(cv_ref, w_ref, b_ref, o_ref):
    a = cv_ref[...]
    a = a * jax.nn.sigmoid(a)
    o_ref[...] = _dot(a.astype(BF16), w_ref[...].astype(BF16)) + b_ref[...]


def _adaln(cv, ada_w, ada_b):
    depth, d, n6 = ada_w.shape
    tn = d // 2
    return pl.pallas_call(
        _adaln_kernel,
        grid=(depth, n6 // tn),
        in_specs=[
            pl.BlockSpec((8, d), lambda l, j: (0, 0)),
            pl.BlockSpec((None, d, tn), lambda l, j: (l, 0, j)),
            pl.BlockSpec((None, 1, tn), lambda l, j: (l, 0, j)),
        ],
        out_specs=pl.BlockSpec((None, 8, tn), lambda l, j: (l, 0, j)),
        out_shape=jax.ShapeDtypeStruct((depth, 8, n6), F32),
        compiler_params=_cparams("arbitrary", "arbitrary"),
        name="adaln",
    )(cv, ada_w, ada_b.reshape(depth, 1, n6))


class _Rows:
    def __init__(self, n_prompt, n_sample, dec_seq):
        self.n_prompt, self.n_sample, self.dec_seq = n_prompt, n_sample, dec_seq
        self.n = n_prompt + n_sample
        assert n_prompt % TM == 0 and dec_seq % TM == 0
        self.tiles = self.n // TM
        self.ptiles = n_prompt // TM
        self.tiles_per_seq = dec_seq // TM

    def mod_row(self, i):
        return jnp.where(i < self.ptiles, 0, 1 + (i - self.ptiles) // self.tiles_per_seq)

    def rope_block(self, i):
        return jnp.where(i < self.ptiles, 0, 1 + (i - self.ptiles) % self.tiles_per_seq)

    def mod_spec(self, d, layer, k):
        base = (layer * N_MOD + k) * 8
        return pl.BlockSpec((None, 1, d), lambda i: (base + self.mod_row(i), 0, 0))

    def rope_spec(self):
        return pl.BlockSpec((TM, LANES), lambda i: (self.rope_block(i), 0))


    def split_specs(self, width):
        return [pl.BlockSpec((TM, width), lambda i: (jnp.minimum(i, self.ptiles - 1), 0)),
                pl.BlockSpec((TM, width), lambda i: (jnp.maximum(i - self.ptiles, 0), 0))]


def _pick_rows(p_ref, s_ref, ptiles):
    return jnp.where(pl.program_id(0) < ptiles, p_ref[...], s_ref[...])


def _row_spec(width):
    return pl.BlockSpec((TM, width), lambda i: (i, 0))


def _full_spec(shape):
    nd = len(shape)
    return pl.BlockSpec(shape, lambda i: (0,) * nd)


def _mla_proj_kernel(*refs, q_lora, kv_lora, heads, split_ptiles):
    if split_ptiles:
        x = _pick_rows(refs[0], refs[1], split_ptiles)
        refs = refs[2:]
    else:
        x = refs[0][...]
        refs = refs[1:]
    nw_ref, sh_ref, sc_ref, win_ref, qn_ref, wuq_ref, kvn_ref, cos_ref, sin_ref, q_ref, kvl_ref = refs
    h = _rms(x, nw_ref[...]) * (1.0 + sc_ref[...]) + sh_ref[...]
    a = _dot(h.astype(BF16), win_ref[...])
    cq = a[:, :q_lora]
    ckv = a[:, q_lora:q_lora + kv_lora]
    kpe = a[:, q_lora + kv_lora:q_lora + kv_lora + LANES]
    cos, sin = cos_ref[...], sin_ref[...]
    q = _dot(_rms(cq, qn_ref[...]).astype(BF16), wuq_ref[...])
    for hh in range(heads):
        lo = hh * 2 * LANES
        q_ref[:, lo:lo + LANES] = q[:, lo:lo + LANES].astype(BF16)
        q_ref[:, lo + LANES:lo + 2 * LANES] = _rope(q[:, lo + LANES:lo + 2 * LANES], cos, sin).astype(BF16)
    kvl_ref[:, :kv_lora] = _rms(ckv, kvn_ref[...])
    kvl_ref[:, kv_lora:] = _rope(kpe, cos, sin)


def _mla_proj(rows, xs, mods, layer, norm_w, w_in_p, q_norm, w_uq_p, kv_norm, cos_t, sin_t):
    d = xs[0].shape[1]
    q_lora, kv_lora = q_norm.shape[-1], kv_norm.shape[-1]
    heads = w_uq_p.shape[1] // (2 * LANES)
    split = len(xs) == 2
    kern = functools.partial(_mla_proj_kernel, q_lora=q_lora, kv_lora=kv_lora, heads=heads,
                             split_ptiles=rows.ptiles if split else 0)
    return pl.pallas_call(
        kern,
        grid=(rows.tiles,),
        in_specs=(rows.split_specs(d) if split else [_row_spec(d)]) + [
            _full_spec((1, d)), rows.mod_spec(d, layer, 0), rows.mod_spec(d, layer, 1),
            _full_spec(w_in_p.shape), _full_spec((1, q_lora)), _full_spec(w_uq_p.shape),
            _full_spec((1, kv_lora)), rows.rope_spec(), rows.rope_spec(),
        ],
        out_specs=[_row_spec(heads * 2 * LANES), _row_spec(kv_lora + LANES)],
        out_shape=[jax.ShapeDtypeStruct((rows.n, heads * 2 * LANES), BF16),
                   jax.ShapeDtypeStruct((rows.n, kv_lora + LANES), F32)],
        compiler_params=_cparams("arbitrary"),
        name="mla_proj",
    )(*xs, norm_w.reshape(1, d), mods, mods, w_in_p, q_norm.reshape(1, -1), w_uq_p, kv_norm.reshape(1, -1),
      cos_t, sin_t)


def _mla_attn_kernel(*refs, n_ctx, kv_lora, scale, hp):
    if n_ctx:
        q_ref, cckv_ref, ckpe_ref, kvl_ref, wukv_ref, o_ref, k_s, v_s = refs
    else:
        q_ref, kvl_ref, wukv_ref, o_ref, k_s, v_s = refs

    hw = MLA_NOPE + MLA_V

    @pl.when(pl.program_id(2) == 0)
    def _():
        lat = kvl_ref[...]
        ckv_lat = lat[:, :kv_lora].astype(BF16)
        kpe_lat = lat[:, kv_lora:].astype(BF16)
        for a in range(hp):
            w = wukv_ref[:, a * hw:(a + 1) * hw]
            kv = _dot(ckv_lat, w)
            k_s[a, n_ctx:, :MLA_NOPE] = kv[:, :MLA_NOPE].astype(BF16)
            k_s[a, n_ctx:, MLA_NOPE:] = kpe_lat
            v_s[a, n_ctx:, :] = kv[:, MLA_NOPE:].astype(BF16)
            if n_ctx:
                kvc = _dot(cckv_ref[...].astype(BF16), w)
                k_s[a, :n_ctx, :MLA_NOPE] = kvc[:, :MLA_NOPE].astype(BF16)
                k_s[a, :n_ctx, MLA_NOPE:MLA_NOPE + MLA_ROPE] = ckpe_ref[...].astype(BF16)
                k_s[a, :n_ctx, MLA_NOPE + MLA_ROPE:] = jnp.zeros((n_ctx, LANES - MLA_ROPE), BF16)
                v_s[a, :n_ctx, :] = kvc[:, MLA_NOPE:].astype(BF16)

    qw = 2 * LANES
    ss = [_dot_nt(q_ref[:, a * qw:(a + 1) * qw], k_s[a]) * (scale * LOG2E) for a in range(hp)]
    for a in range(hp):
        s = ss[a]
        m = jnp.max(s, axis=-1, keepdims=True)
        p = jnp.exp2(s - m)
        l = jnp.sum(p, axis=-1, keepdims=True)
        o_ref[:, a * MLA_V:(a + 1) * MLA_V] = (_dot(p.astype(BF16), v_s[a]) / l).astype(BF16)


def _mla_attn(q, kvl, w_ukv_b, row0, nb, seq, heads, ctx=None):
    kv_lora = kvl.shape[1] - LANES
    tq = min(MLA_TQ, seq)
    nq = seq // tq
    assert row0 % seq == 0
    seq0, tile0 = row0 // seq, row0 // tq
    n_ctx = 0 if ctx is None else ctx[0].shape[2]
    lk = n_ctx + seq
    scale = 1.0 / math.sqrt(MLA_NOPE + MLA_ROPE)
    hp = math.gcd(heads, MLA_HEADS_PER_STEP)
    in_specs = [pl.BlockSpec((tq, hp * 2 * LANES), lambda b, h, i: (tile0 + b * nq + i, h))]
    args = [q]
    if ctx is not None:
        cckv, ckpe, j = ctx
        in_specs += [pl.BlockSpec((None, None, n_ctx, kv_lora), lambda b, h, i: (b, j, 0, 0)),
                     pl.BlockSpec((None, None, n_ctx, MLA_ROPE), lambda b, h, i: (b, j, 0, 0))]
        args += [cckv, ckpe]
    in_specs += [pl.BlockSpec((seq, kv_lora + LANES), lambda b, h, i: (seq0 + b, 0)),
                 pl.BlockSpec((kv_lora, hp * (MLA_NOPE + MLA_V)), lambda b, h, i: (0, h))]
    args += [kvl, w_ukv_b]
    kern = functools.partial(_mla_attn_kernel, n_ctx=n_ctx, kv_lora=kv_lora, scale=scale, hp=hp)
    return pl.pallas_call(
        kern,
        grid=(nb, heads // hp, nq),
        in_specs=in_specs,
        out_specs=pl.BlockSpec((tq, hp * MLA_V), lambda b, h, i: (b * nq + i, h)),
        out_shape=jax.ShapeDtypeStruct((nb * seq, heads * MLA_V), BF16),
        scratch_shapes=[pltpu.VMEM((hp, lk, 2 * LANES), BF16), pltpu.VMEM((hp, lk, MLA_V), BF16)],
        compiler_params=_cparams("arbitrary", "arbitrary", "arbitrary"),
        name="mla_attn_ctx" if ctx is None else "mla_attn_lat",
    )(*args)


def _swa_proj_kernel(x_ref, nw_ref, sh_ref, sc_ref, w_ref, cos_ref, sin_ref, q_ref, kv_ref, *, nq, nk):
    x = x_ref[...]
    h = _rms(x, nw_ref[...]) * (1.0 + sc_ref[...]) + sh_ref[...]
    a = _dot(h.astype(BF16), w_ref[...])
    qk = _rope(a[:, :nq + nk], cos_ref[...], sin_ref[...])
    q_ref[...] = qk[:, :nq].astype(BF16)
    kv_ref[:, :nk] = qk[:, nq:]
    kv_ref[:, nk:] = a[:, nq + nk:]


def _swa_proj(rows, x, mods, layer, norm_w, w_qkv_b, nq, nk, cos_t, sin_t):
    d = x.shape[1]
    kern = functools.partial(_swa_proj_kernel, nq=nq, nk=nk)
    return pl.pallas_call(
        kern,
        grid=(rows.tiles,),
        in_specs=[_row_spec(d), _full_spec((1, d)), rows.mod_spec(d, layer, 0), rows.mod_spec(d, layer, 1),
                  _full_spec(w_qkv_b.shape), rows.rope_spec(), rows.rope_spec()],
        out_specs=[_row_spec(nq), _row_spec(2 * nk)],
        out_shape=[jax.ShapeDtypeStruct((rows.n, nq), BF16), jax.ShapeDtypeStruct((rows.n, 2 * nk), F32)],
        compiler_params=_cparams("arbitrary"),
        name="swa_proj",
    )(x, norm_w.reshape(1, d), mods, mods, w_qkv_b, cos_t, sin_t)


def _sink_softmax_pv(s_parts, v_parts, sink):
    m = jnp.maximum(functools.reduce(jnp.maximum, [jnp.max(s, axis=-1, keepdims=True) for s in s_parts]), sink)
    l = jnp.exp2(sink - m)
    o = None
    for s, v in zip(s_parts, v_parts):
        p = jnp.exp2(s - m)
        l = l + jnp.sum(p, axis=-1, keepdims=True)
        pv = _dot(p.astype(BF16), v)
        o = pv if o is None else o + pv
    return o / l


def _stack_heads(q, g, group, dh):
    return jnp.concatenate([q[:, (g * group + t) * dh:(g * group + t + 1) * dh] for t in range(group)], axis=0)


def _sink_column(sink_ref, g, group, rows):
    return jnp.concatenate([jnp.full((rows, 1), sink_ref[g * group + t] * LOG2E, F32) for t in range(group)], axis=0)


def _swa_ctx_kernel(sink_ref, q_ref, kv_ref, o_ref, *, hq, hkv):
    dh = SWA_HEAD_DIM
    group = hq // hkv
    c = LOG2E / math.sqrt(dh)
    rows = q_ref.shape[0]
    q = q_ref[...]
    kv = kv_ref[...].astype(BF16)
    logits = [_dot_nt(_stack_heads(q, g, group, dh), kv[:, g * dh:(g + 1) * dh]) * c for g in range(hkv)]
    for g in range(hkv):
        v = kv[:, (hkv + g) * dh:(hkv + g + 1) * dh]
        o = _sink_softmax_pv([logits[g]], [v], _sink_column(sink_ref, g, group, rows)).astype(BF16)
        for t in range(group):
            hh = g * group + t
            o_ref[:, hh * dh:(hh + 1) * dh] = o[t * rows:(t + 1) * rows, :]


def _swa_ctx(q, kv, sink, nb, seq, hq, hkv):
    kern = functools.partial(_swa_ctx_kernel, hq=hq, hkv=hkv)
    return pl.pallas_call(
        kern,
        grid=(nb,),
        in_specs=[pl.BlockSpec(memory_space=pltpu.SMEM),
                  pl.BlockSpec((seq, hq * SWA_HEAD_DIM), lambda b: (b, 0)),
                  pl.BlockSpec((seq, 2 * hkv * SWA_HEAD_DIM), lambda b: (b, 0))],
        out_specs=pl.BlockSpec((seq, hq * SWA_HEAD_DIM), lambda b: (b, 0)),
        out_shape=jax.ShapeDtypeStruct((nb * seq, hq * SWA_HEAD_DIM), BF16),
        compiler_params=_cparams("arbitrary"),
        name="swa_attn_ctx",
    )(sink, q, kv)


def _swa_lat_kernel(sink_ref, q_ref, kc_ref, vc_ref, kv0_ref, kv1_ref, kv2_ref, o_ref, *, hq, hkv, seq):
    dh = SWA_HEAD_DIM
    group = hq // hkv
    c = LOG2E / math.sqrt(dh)
    blk = SWA_BLOCK
    i = pl.program_id(1)
    qpos = i * blk + lax.broadcasted_iota(I32, (group * blk, 3 * blk), 0) % blk
    kpos = (i - 1) * blk + lax.broadcasted_iota(I32, (group * blk, 3 * blk), 1)
    band = (jnp.abs(qpos - kpos) <= SWA_WINDOW) & (kpos >= 0) & (kpos < seq)
    q = q_ref[...]
    kc = kc_ref[...].astype(BF16)
    vc = vc_ref[...].astype(BF16)
    kvb = jnp.concatenate([kv0_ref[...], kv1_ref[...], kv2_ref[...]], axis=0).astype(BF16)
    logits = []
    for g in range(hkv):
        qg = _stack_heads(q, g, group, dh)
        s_c = _dot_nt(qg, kc[:, g * dh:(g + 1) * dh]) * c
        s_b = jnp.where(band, _dot_nt(qg, kvb[:, g * dh:(g + 1) * dh]) * c, NEG_INF)
        logits.append((s_c, s_b))
    for g in range(hkv):
        v_c = vc[:, g * dh:(g + 1) * dh]
        v_b = kvb[:, (hkv + g) * dh:(hkv + g + 1) * dh]
        o = _sink_softmax_pv(list(logits[g]), [v_c, v_b], _sink_column(sink_ref, g, group, blk)).astype(BF16)
        for t in range(group):
            hh = g * group + t
            o_ref[:, hh * dh:(hh + 1) * dh] = o[t * blk:(t + 1) * blk, :]


def _swa_lat(q, kv, sink, cache_k, cache_v, j, row0, nb, seq, hq, hkv):
    blk = SWA_BLOCK
    nblk = seq // blk
    blk0 = row0 // blk
    n_ctx = cache_k.shape[2]
    wq, wkv = hq * SWA_HEAD_DIM, hkv * SWA_HEAD_DIM
    kern = functools.partial(_swa_lat_kernel, hq=hq, hkv=hkv, seq=seq)

    def band_spec(off):
        return pl.BlockSpec((blk, 2 * wkv), lambda b, i: (blk0 + b * nblk + jnp.clip(i + off, 0, nblk - 1), 0))

    return pl.pallas_call(
        kern,
        grid=(nb, nblk),
        in_specs=[pl.BlockSpec(memory_space=pltpu.SMEM),
                  pl.BlockSpec((blk, wq), lambda b, i: (blk0 + b * nblk + i, 0)),
                  pl.BlockSpec((None, None, n_ctx, wkv), lambda b, i: (b, j, 0, 0)),
                  pl.BlockSpec((None, None, n_ctx, wkv), lambda b, i: (b, j, 0, 0)),
                  band_spec(-1), band_spec(0), band_spec(1)],
        out_specs=pl.BlockSpec((blk, wq), lambda b, i: (b * nblk + i, 0)),
        out_shape=jax.ShapeDtypeStruct((nb * seq, wq), BF16),
        compiler_params=_cparams("arbitrary", "arbitrary"),
        name="swa_attn_lat",
    )(sink, q, cache_k, cache_v, kv, kv, kv)


def _attn_out_kernel(*refs, n_exp, ptiles, tiles_per_seq, split_x):
    o = _pick_rows(refs[0], refs[1], ptiles)
    if split_x:
        x = _pick_rows(refs[2], refs[3], ptiles)
        refs = refs[4:]
    else:
        x = refs[2][...]
        refs = refs[3:]
    wo_ref, gate_ref, nw_ref, sh_ref, sc_ref, wr_ref, wrt_ref, x1_ref, hx_ref, a2_ref = refs
    half = x.shape[1] // 2
    x1 = x + gate_ref[...] * _dot(o, wo_ref[...])
    x1_ref[...] = x1
    h = _rms(x1, nw_ref[...]) * (1.0 + sc_ref[...]) + sh_ref[...]
    hb = h.astype(BF16)
    bits = lax.bitcast_convert_type(hb.astype(F32), U32)
    hx_ref[:, :half] = (bits[:, half:] & U32(HI16)) | (bits[:, :half] >> U32(16))
    lane = lax.broadcasted_iota(I32, (TM, EXT), 1)
    logits = jnp.where(lane < n_exp, _dot(hb, wr_ref[...]), NEG_INF)
    e = jnp.exp(logits - jnp.max(logits, axis=-1, keepdims=True))
    aff = e / jnp.sum(e, axis=-1, keepdims=True)
    i = pl.program_id(0)
    mod_row = jnp.where(i < ptiles, 0, 1 + (i - ptiles) // tiles_per_seq).astype(F32)
    hx_ref[:, half:] = lax.bitcast_convert_type(jnp.where(lane == n_exp, mod_row, aff), U32)
    lt = _dot_nt(wrt_ref[...], hb)
    et = jnp.exp(lt - jnp.max(lt, axis=0, keepdims=True))
    at = et / jnp.sum(et, axis=0, keepdims=True)
    a2_ref[...] = jnp.concatenate([at[:, c * CHUNK:(c + 1) * CHUNK] for c in range(TM // CHUNK)], axis=0)


def _attn_out(rows, o_p, o_s, xs, w_o_b, mods, layer, norm_w, w_router_p, w_router_t):
    d = xs[0].shape[1]
    n_exp = w_router_t.shape[0]
    split_x = len(xs) == 2
    kern = functools.partial(_attn_out_kernel, n_exp=n_exp, ptiles=rows.ptiles, tiles_per_seq=rows.tiles_per_seq,
                             split_x=split_x)
    cpt = TM // CHUNK
    return pl.pallas_call(
        kern,
        grid=(rows.tiles,),
        in_specs=rows.split_specs(o_p.shape[1]) + (rows.split_specs(d) if split_x else [_row_spec(d)]) + [
            _full_spec(w_o_b.shape), rows.mod_spec(d, layer, 2),
            _full_spec((1, d)), rows.mod_spec(d, layer, 3), rows.mod_spec(d, layer, 4),
            _full_spec(w_router_p.shape), _full_spec(w_router_t.shape)],
        out_specs=[_row_spec(d), _row_spec(d // 2 + EXT), pl.BlockSpec((cpt * n_exp, CHUNK), lambda i: (i, 0))],
        out_shape=[jax.ShapeDtypeStruct((rows.n, d), F32), jax.ShapeDtypeStruct((rows.n, d // 2 + EXT), U32),
                   jax.ShapeDtypeStruct((rows.n // CHUNK * n_exp, CHUNK), F32)],
        compiler_params=_cparams("arbitrary"),
        name="attn_out",
    )(o_p, o_s, *xs, w_o_b, mods, norm_w.reshape(1, d), mods, mods, w_router_p, w_router_t)


def _select_group(a3, cap):
    nj, n_exp, _ = a3.shape
    bits = lax.bitcast_convert_type(a3, I32)
    capf = jnp.float32(cap)

    def count(pred):
        c = jnp.sum(pred.astype(F32), axis=0)
        return jnp.sum(c, axis=-1, keepdims=True)[None]

    def thr_step(k, t):
        cand = t | lax.shift_left(jnp.int32(1), 30 - k)
        return jnp.where(count(bits >= cand) >= capf, cand, t)

    thr = lax.fori_loop(0, 31, thr_step, jnp.zeros((1, n_exp, 1), I32))
    gt = bits > thr
    eq = bits == thr
    need = capf - count(gt)
    tok = (lax.broadcasted_iota(I32, a3.shape, 0) * CHUNK + lax.broadcasted_iota(I32, a3.shape, 2))
    nbits = max(1, (nj * CHUNK).bit_length())

    def tie_step(k, x):
        cand = x | lax.shift_left(jnp.int32(1), nbits - 1 - k)
        return jnp.where(count(eq & (tok < cand)) <= need, cand, x)

    bound = lax.fori_loop(0, nbits, tie_step, jnp.zeros((1, n_exp, 1), I32))
    return (gt | (eq & (tok < bound))).astype(F32)


def _compact(mask_e, cap, tri, lstrict):
    cl = _dot(mask_e.astype(BF16), tri)
    cnt = cl[:, CHUNK - 1:CHUNK]
    off = _dot(lstrict, jnp.broadcast_to(cnt, (LANES, LANES)).astype(BF16))[:, :1]
    ends = off + cnt
    r = lax.broadcasted_iota(I32, (1, cap), 1).astype(F32)
    jstar = jnp.sum((ends <= r).astype(F32), axis=0, keepdims=True)
    onehot = lax.broadcasted_iota(I32, (LANES, cap), 0).astype(F32) == jstar
    offsel = jnp.sum(jnp.where(onehot, off, 0.0), axis=0, keepdims=True)
    g = _dot(cl.T.astype(BF16), onehot.astype(BF16))
    local = jnp.sum((g <= r - offsel).astype(F32), axis=0, keepdims=True)
    return (jstar * CHUNK + local).astype(I32)


def _moe_select_kernel(a2_ref, tri_ref, ls_ref, idx_ref, m_s, *, n_exp, groups):
    tri = tri_ref[...]
    lstrict = ls_ref[...]
    col = 0
    for (chunk0, nj, cap) in groups:
        m_s[...] = jnp.zeros(m_s.shape, F32)
        a3 = a2_ref[chunk0 * n_exp:(chunk0 + nj) * n_exp, :].reshape(nj, n_exp, CHUNK)
        m_s[:nj * n_exp, :] = _select_group(a3, cap).reshape(nj * n_exp, CHUNK)
        for e in range(n_exp):
            mask_e = m_s[pl.ds(e, LANES, stride=n_exp), :]
            idx_ref[e:e + 1, col:col + cap] = _compact(mask_e, cap, tri, lstrict) + chunk0 * CHUNK
        col += cap


def _moe_select(a2, n_exp, groups):
    total = sum(g[2] for g in groups)
    assert all(g[1] <= LANES and g[2] % LANES == 0 for g in groups)
    ii = jnp.arange(LANES)
    tri = (ii[:, None] <= ii[None, :]).astype(BF16)
    lstrict = (ii[None, :] < ii[:, None]).astype(BF16)
    kern = functools.partial(_moe_select_kernel, n_exp=n_exp, groups=groups)
    return pl.pallas_call(
        kern,
        grid=(1,),
        in_specs=[_full_spec(a2.shape), _full_spec((LANES, LANES)), _full_spec((LANES, LANES))],
        out_specs=_full_spec((n_exp, total)),
        out_shape=jax.ShapeDtypeStruct((n_exp, total), I32),
        scratch_shapes=[pltpu.VMEM((LANES * n_exp, CHUNK), F32)],
        compiler_params=_cparams("arbitrary"),
        name="moe_select",
    )(a2, tri, lstrict)


def _moe_ffn_kernel(idx_ref, hx_hbm, x_in_hbm, gates_ref, wg_ref, wu_ref, wd_ref, x_hbm,
                    xs, acc, buf, sem_h, sem_g, sem_s, *, rows_e, rt, mt, d, n_fc, n_mods):
    del x_in_hbm
    e = pl.program_id(0)
    fc = pl.program_id(1)
    n_exp = pl.num_programs(0)
    half = d // 2
    n_t = rows_e // rt
    slot = e % 2

    def hbm_row(ref, tok):
        return ref.at[tok >> 3, pl.ds(tok & (SUBLANES - 1), 1), :]

    def h_gather(ex, sl):
        def body(r8, c):
            for s in range(SUBLANES):
                tok = idx_ref[ex * rows_e + r8 * SUBLANES + s]
                pltpu.make_async_copy(hbm_row(hx_hbm, tok), xs.at[sl, r8, pl.ds(s, 1), :], sem_h.at[sl]).start()
            return c
        lax.fori_loop(0, rows_e // SUBLANES, body, 0)

    def x_rows(t, b, to_vmem):
        def body(r8, c):
            for s in range(SUBLANES):
                tok = idx_ref[e * rows_e + t * rt + r8 * SUBLANES + s]
                hbm, vmem = hbm_row(x_hbm, tok), buf.at[b, r8, pl.ds(s, 1), :]
                if to_vmem:
                    pltpu.make_async_copy(hbm, vmem, sem_g.at[b]).start()
                else:
                    pltpu.make_async_copy(vmem, hbm, sem_s.at[b]).start()
            return c
        lax.fori_loop(0, rt // SUBLANES, body, 0)

    def x_wait(b, to_vmem):
        if to_vmem:
            pltpu.make_async_copy(x_hbm.at[pl.ds(0, rt // SUBLANES)], buf.at[b], sem_g.at[b]).wait()
        else:
            pltpu.make_async_copy(buf.at[b], x_hbm.at[pl.ds(0, rt // SUBLANES)], sem_s.at[b]).wait()

    @pl.when((fc == 0) & (e == 0))
    def _():
        h_gather(0, 0)

    @pl.when(fc == 0)
    def _():
        pltpu.make_async_copy(hx_hbm.at[pl.ds(0, rows_e // SUBLANES)], xs.at[slot], sem_h.at[slot]).wait()
        acc[...] = jnp.zeros(acc.shape, F32)

    tail = sorted({t % 2 for t in range(max(n_t - 2, 0), n_t)})

    @pl.when((fc == n_fc - 1) & (e > 0))
    def _():
        for b in tail:
            x_wait(b, False)

    @pl.when(fc == n_fc - 1)
    def _():
        x_rows(0, 0, True)

    wg = wg_ref[...].astype(BF16)
    wu = wu_ref[...].astype(BF16)
    wd = wd_ref[...].astype(BF16)
    ow = min(2 * LANES, d)
    n_mt = rows_e // mt
    g8 = rows_e // SUBLANES // (n_fc * n_mt)
    nxt = (e + 1) % n_exp
    for t in range(n_mt):
        for k in range(g8):
            r8 = (fc * n_mt + t) * g8 + k
            for s in range(SUBLANES):
                tok = idx_ref[nxt * rows_e + r8 * SUBLANES + s]
                pltpu.make_async_copy(hbm_row(hx_hbm, tok), xs.at[1 - slot, r8, pl.ds(s, 1), :],
                                      sem_h.at[1 - slot]).start()
        w = xs[slot, t * mt // SUBLANES:(t + 1) * mt // SUBLANES, :, :half].reshape(mt, half)
        x = jnp.concatenate([lax.bitcast_convert_type(w << U32(16), F32).astype(BF16),
                             lax.bitcast_convert_type(w & U32(HI16), F32).astype(BF16)], axis=-1)
        gate = _dot(x, wg)
        hid = (gate * jax.nn.sigmoid(gate) * _dot(x, wu)).astype(BF16)
        for c in range(d // ow):
            acc[t * mt:(t + 1) * mt, c * ow:(c + 1) * ow] += _dot(hid, wd[:, c * ow:(c + 1) * ow])

    @pl.when(fc == n_fc - 1)
    def _():
        gates = gates_ref[...]
        lane = lax.broadcasted_iota(I32, (rt, EXT), 1)
        for t in range(n_t):
            b = t % 2
            if t + 1 < n_t:
                if t >= 1:
                    x_wait(1 - b, False)
                x_rows(t + 1, 1 - b, True)
            ext = xs[slot, t * rt // SUBLANES:(t + 1) * rt // SUBLANES, :, half:].reshape(rt, EXT)
            ext = lax.bitcast_convert_type(ext, F32)
            g = jnp.sum(jnp.where(lane == e, ext, 0.0), axis=-1, keepdims=True)
            mrow = jnp.sum(jnp.where(lane == n_exp, ext, 0.0), axis=-1, keepdims=True)
            gate_rows = jnp.broadcast_to(gates[0:1, :], (rt, d))
            for m in range(1, n_mods):
                gate_rows = jnp.where(mrow == float(m), gates[m:m + 1, :], gate_rows)
            upd = acc[t * rt:(t + 1) * rt, :] * g * gate_rows
            x_wait(b, True)
            buf[b] = buf[b] + upd.reshape(rt // SUBLANES, SUBLANES, d)
            x_rows(t, b, False)

    @pl.when((fc == n_fc - 1) & (e == n_exp - 1))
    def _():
        for b in tail:
            x_wait(b, False)
        pltpu.make_async_copy(hx_hbm.at[pl.ds(0, rows_e // SUBLANES)], xs.at[1 - slot], sem_h.at[1 - slot]).wait()


def _moe_ffn(idx_flat, hx, x1, gates, n_mods, w_gate, w_up, w_down, layer, rows_e):
    n, d = x1.shape
    _, n_exp, _, f = w_gate.shape
    fcw = min(256, f)
    n_fc = f // fcw
    rt = math.gcd(rows_e // 2, TM)
    mt = math.gcd(rows_e, 2 * TM)
    wx = hx.shape[1]
    kern = functools.partial(_moe_ffn_kernel, rows_e=rows_e, rt=rt, mt=mt, d=d, n_fc=n_fc, n_mods=n_mods)
    grid_spec = pltpu.PrefetchScalarGridSpec(
        num_scalar_prefetch=1,
        grid=(n_exp, n_fc),
        in_specs=[
            pl.BlockSpec(memory_space=pl.ANY),
            pl.BlockSpec(memory_space=pl.ANY),
            pl.BlockSpec((8, d), lambda e, c, idx: (0, 0)),
            pl.BlockSpec((None, None, d, fcw), lambda e, c, idx: (layer, e, 0, c)),
            pl.BlockSpec((None, None, d, fcw), lambda e, c, idx: (layer, e, 0, c)),
            pl.BlockSpec((None, None, fcw, d), lambda e, c, idx: (layer, e, c, 0)),
        ],
        out_specs=pl.BlockSpec(memory_space=pl.ANY),
        scratch_shapes=[pltpu.VMEM((2, rows_e // SUBLANES, SUBLANES, wx), U32), pltpu.VMEM((rows_e, d), F32),
                        pltpu.VMEM((2, rt // SUBLANES, SUBLANES, d), F32), pltpu.SemaphoreType.DMA((2,)),
                        pltpu.SemaphoreType.DMA((2,)), pltpu.SemaphoreType.DMA((2,))],
    )
    out = pl.pallas_call(
        kern,
        grid_spec=grid_spec,
        out_shape=jax.ShapeDtypeStruct((n // SUBLANES, SUBLANES, d), F32),
        input_output_aliases={2: 0},
        compiler_params=_cparams("arbitrary", "arbitrary"),
        name="moe_ffn",
    )(idx_flat, hx.reshape(n // SUBLANES, SUBLANES, wx), x1.reshape(n // SUBLANES, SUBLANES, d), gates,
      w_gate, w_up, w_down)
    return out.reshape(n, d)


def _final_norm_kernel(x_ref, w_ref, o_ref):
    o_ref[...] = _rms(x_ref[...], w_ref[...])


def _final_norm(x, w, row0, n_rows):
    d = x.shape[1]
    t0 = row0 // TM
    return pl.pallas_call(
        _final_norm_kernel,
        grid=(n_rows // TM,),
        in_specs=[pl.BlockSpec((TM, d), lambda i: (t0 + i, 0)), _full_spec((1, d))],
        out_specs=_row_spec(d),
        out_shape=jax.ShapeDtypeStruct((n_rows, d), F32),
        compiler_params=_cparams("arbitrary"),
        name="final_norm",
    )(x, w.reshape(1, d))


def _rope_tables(seq):
    rows = seq // GRID_W
    row = jnp.repeat(jnp.arange(rows), GRID_W).astype(F32)
    col = jnp.tile(jnp.arange(GRID_W), rows).astype(F32)
    nf = MLA_ROPE // 4
    inv = ROPE_BASE ** (-jnp.arange(nf, dtype=F32) / nf)
    ang = jnp.concatenate([row[:, None] * inv[None], col[:, None] * inv[None]], axis=-1)
    c, s = jnp.cos(ang), jnp.sin(ang)
    cos = jnp.concatenate([c, c, c, c], axis=-1)
    sin = jnp.concatenate([-s, s, -s, s], axis=-1)
    cos = jnp.concatenate([jnp.ones((TM, LANES), F32), cos], axis=0)
    sin = jnp.concatenate([jnp.zeros((TM, LANES), F32), sin], axis=0)
    return cos, sin


def kernel(x_prompt, x_sample, cache_ckv, cache_kpe, cache_k, cache_v, c, c_ctx, ada_w, ada_b, norm_mix, norm_ffn,
           mla_w_in, mla_q_norm, mla_w_uq, mla_kv_norm, mla_w_ukv, mla_w_o, swa_w_qkv, swa_sink, swa_w_o,
           moe_router, moe_w_gate, moe_w_up, moe_w_down, final_norm):
    assert MLA_ROPE == SWA_HEAD_DIM
    batch, seq, d = x_prompt.shape
    dec_batch, dec_seq, _ = x_sample.shape
    depth = ada_w.shape[0]
    n_p, n_s = batch * seq, dec_batch * dec_seq
    rows = _Rows(n_p, n_s, dec_seq)
    assert seq == TM and n_p % dec_seq == 0
    mla_heads = mla_w_uq.shape[-1] // (MLA_NOPE + MLA_ROPE)
    q_lora, kv_lora = mla_q_norm.shape[-1], mla_kv_norm.shape[-1]
    hq, hkv = swa_sink.shape[-1], cache_k.shape[3]
    n_exp = moe_router.shape[-1]
    cap_p, cap_s = EC_CAPACITY * n_p // n_exp, EC_CAPACITY * n_s // n_exp
    groups = ((0, n_p // CHUNK, cap_p), (n_p // CHUNK, n_s // CHUNK, cap_s))

    cv = jnp.zeros((8, d), F32).at[0].set(c_ctx).at[1:1 + dec_batch].set(c)
    mods = _adaln(cv, ada_w, ada_b)
    mods_r = mods.reshape(depth, 8, N_MOD, d).transpose(0, 2, 1, 3)
    mods_rows = mods_r.reshape(depth * N_MOD * 8, 1, d)
    cos_t, sin_t = _rope_tables(dec_seq)

    xs = (x_prompt.reshape(n_p, d), x_sample.reshape(n_s, d))
    new_ckv, new_kpe, new_k, new_v = [], [], [], []
    for i in range(depth):
        j = i // 2
        if i % 2 == 0:
            w_in = mla_w_in[j]
            w_in_p = jnp.pad(w_in, ((0, 0), (0, q_lora + kv_lora + LANES - w_in.shape[1]))).astype(BF16)
            w_uq = mla_w_uq[j].reshape(q_lora, mla_heads, MLA_NOPE + MLA_ROPE)
            w_uq_p = jnp.pad(w_uq, ((0, 0), (0, 0), (0, 2 * LANES - MLA_NOPE - MLA_ROPE)))
            w_uq_p = w_uq_p.reshape(q_lora, mla_heads * 2 * LANES).astype(BF16)
            q, kvl = _mla_proj(rows, xs, mods_rows, i, norm_mix[i], w_in_p, mla_q_norm[j], w_uq_p,
                               mla_kv_norm[j], cos_t, sin_t)
            w_ukv_b = mla_w_ukv[j].astype(BF16)
            o_p = _mla_attn(q, kvl, w_ukv_b, 0, batch, seq, mla_heads)
            o_s = _mla_attn(q, kvl, w_ukv_b, n_p, dec_batch, dec_seq, mla_heads, ctx=(cache_ckv, cache_kpe, j))
            new_ckv.append(kvl[:n_p, :kv_lora].reshape(batch, seq, kv_lora))
            new_kpe.append(kvl[:n_p, kv_lora:kv_lora + MLA_ROPE].reshape(batch, seq, MLA_ROPE))
            w_o_b = mla_w_o[j].astype(BF16)
        else:
            nq, nk = hq * SWA_HEAD_DIM, hkv * SWA_HEAD_DIM
            q, kv = _swa_proj(rows, xs[0], mods_rows, i, norm_mix[i], swa_w_qkv[j].astype(BF16), nq, nk,
                              cos_t, sin_t)
            o_p = _swa_ctx(q, kv, swa_sink[j], batch, seq, hq, hkv)
            ck = cache_k.reshape(cache_k.shape[:3] + (nk,))
            cvv = cache_v.reshape(cache_v.shape[:3] + (nk,))
            o_s = _swa_lat(q, kv, swa_sink[j], ck, cvv, j, n_p, dec_batch, dec_seq, hq, hkv)
            new_k.append(kv[:n_p, :nk].reshape(batch, seq, hkv, SWA_HEAD_DIM))
            new_v.append(kv[:n_p, nk:].reshape(batch, seq, hkv, SWA_HEAD_DIM))
            w_o_b = swa_w_o[j].astype(BF16)
        w_r = moe_router[i]
        w_router_p = jnp.pad(w_r, ((0, 0), (0, EXT - n_exp))).astype(BF16)
        x1, hx, a2 = _attn_out(rows, o_p, o_s, xs, w_o_b, mods_rows, i, norm_ffn[i], w_router_p,
                               w_r.T.astype(BF16))
        idx = _moe_select(a2, n_exp, groups)
        gates = mods_r[i, N_MOD - 1]
        xs = (_moe_ffn(idx.reshape(-1), hx, x1, gates, 1 + dec_batch, moe_w_gate, moe_w_up, moe_w_down, i,
                       cap_p + cap_s),)
    x = xs[0]
    y_prompt = _final_norm(x, final_norm, 0, n_p).reshape(batch, seq, d)
    y_sample = _final_norm(x, final_norm, n_p, n_s).reshape(dec_batch, dec_seq, d)
    return (y_prompt, y_sample, jnp.stack(new_ckv, axis=1), jnp.stack(new_kpe, axis=1),
            jnp.stack(new_k, axis=1), jnp.stack(new_v, axis=1))
```

```python
import functools
import math

import jax
import jax.numpy as jnp
from jax import lax
from jax.experimental import pallas as pl
from jax.experimental.pallas import tpu as pltpu

F32 = jnp.float32
BF16 = jnp.bfloat16
I32 = jnp.int32
U32 = jnp.uint32
HI16 = 0xFFFF0000

GRID_W = 64
RMS_EPS = 1e-6
ROPE_BASE = 10000.0
NEG_INF = -1e30
N_MOD = 6
MLA_NOPE = 128
MLA_ROPE = 64
MLA_V = 128
SWA_HEAD_DIM = 64
SWA_WINDOW = 128
SWA_BLOCK = 128
EC_CAPACITY = 2

LANES = 128
SUBLANES = 8
TM = 256
MLA_TQ = 256
MLA_HEADS_PER_STEP = 4
LOG2E = math.log2(math.e)
CHUNK = 128
EXT = 128
Y_PAD = 16
COMBINE_GROUP = 4
VMEM_LIMIT = 52 * 1024 * 1024


def _cparams(*sem):
    return pltpu.CompilerParams(dimension_semantics=sem, vmem_limit_bytes=VMEM_LIMIT)


def _rms(x, w):
    return x * lax.rsqrt(jnp.mean(x * x, axis=-1, keepdims=True) + RMS_EPS) * w


def _swap_halves(x):
    w = x.shape[-1]
    lane = lax.broadcasted_iota(I32, x.shape, x.ndim - 1)
    return jnp.where(lane % 64 < 32, pltpu.roll(x, w - 32, x.ndim - 1), pltpu.roll(x, 32, x.ndim - 1))


def _rope(x, cos, sin):
    n = x.shape[-1] // LANES
    cw = jnp.concatenate([cos] * n, axis=-1) if n > 1 else cos
    sw = jnp.concatenate([sin] * n, axis=-1) if n > 1 else sin
    return x * cw + _swap_halves(x) * sw


def _dot(a, b):
    return jnp.dot(a, b, preferred_element_type=F32)


def _dot_nt(a, b):
    return lax.dot_general(a, b, (((1,), (1,)), ((), ())), preferred_element_type=F32)


def _adaln_kernel(cv_ref, w_ref, b_ref, o_ref):
    a = cv_ref[...]
    a = a * jax.nn.sigmoid(a)
    o_ref[...] = _dot(a.astype(BF16), w_ref[...].astype(BF16)) + b_ref[...]


def _adaln(cv, ada_w, ada_b):
    depth, d, n6 = ada_w.shape
    tn = d // 2
    return pl.pallas_call(
        _adaln_kernel,
        grid=(depth, n6 // tn),
        in_specs=[
            pl.BlockSpec((8, d), lambda l, j: (0, 0)),
            pl.BlockSpec((None, d, tn), lambda l, j: (l, 0, j)),
            pl.BlockSpec((None, 1, tn), lambda l, j: (l, 0, j)),
        ],
        out_specs=pl.BlockSpec((None, 8, tn), lambda l, j: (l, 0, j)),
        out_shape=jax.ShapeDtypeStruct((depth, 8, n6), F32),
        compiler_params=_cparams("arbitrary", "arbitrary"),
        name="adaln",
    )(cv, ada_w, ada_b.reshape(depth, 1, n6))


class _Rows:
    def __init__(self, n_prompt, n_sample, dec_seq):
        self.n_prompt, self.n_sample, self.dec_seq = n_prompt, n_sample, dec_seq
        self.n = n_prompt + n_sample
        assert n_prompt % TM == 0 and dec_seq % TM == 0
        self.tiles = self.n // TM
        self.ptiles = n_prompt // TM
        self.tiles_per_seq = dec_seq // TM

    def mod_row(self, i):
        return jnp.where(i < self.ptiles, 0, 1 + (i - self.ptiles) // self.tiles_per_seq)

    def rope_block(self, i):
        return jnp.where(i < self.ptiles, 0, 1 + (i - self.ptiles) % self.tiles_per_seq)

    def mod_spec(self, d, layer, k):
        base = (layer * N_MOD + k) * 8
        return pl.BlockSpec((None, 1, d), lambda i: (base + self.mod_row(i), 0, 0))

    def rope_spec(self):
        return pl.BlockSpec((TM, LANES), lambda i: (self.rope_block(i), 0))


    def split_specs(self, width):
        return [pl.BlockSpec((TM, width), lambda i: (jnp.minimum(i, self.ptiles - 1), 0)),
                pl.BlockSpec((TM, width), lambda i: (jnp.maximum(i - self.ptiles, 0), 0))]


def _pick_rows(p_ref, s_ref, ptiles):
    return jnp.where(pl.program_id(0) < ptiles, p_ref[...], s_ref[...])


def _row_spec(width):
    return pl.BlockSpec((TM, width), lambda i: (i, 0))


def _full_spec(shape):
    nd = len(shape)
    return pl.BlockSpec(shape, lambda i: (0,) * nd)


def _mla_proj_kernel(*refs, q_lora, kv_lora, heads, split_ptiles):
    if split_ptiles:
        x = _pick_rows(refs[0], refs[1], split_ptiles)
        refs = refs[2:]
    else:
        x = refs[0][...]
        refs = refs[1:]
    nw_ref, sh_ref, sc_ref, win_ref, qn_ref, wuq_ref, kvn_ref, cos_ref, sin_ref, q_ref, kvl_ref = refs
    h = _rms(x, nw_ref[...]) * (1.0 + sc_ref[...]) + sh_ref[...]
    a = _dot(h.astype(BF16), win_ref[...])
    cq = a[:, :q_lora]
    ckv = a[:, q_lora:q_lora + kv_lora]
    kpe = a[:, q_lora + kv_lora:q_lora + kv_lora + LANES]
    cos, sin = cos_ref[...], sin_ref[...]
    q = _dot(_rms(cq, qn_ref[...]).astype(BF16), wuq_ref[...])
    for hh in range(heads):
        lo = hh * 2 * LANES
        q_ref[:, lo:lo + LANES] = q[:, lo:lo + LANES].astype(BF16)
        q_ref[:, lo + LANES:lo + 2 * LANES] = _rope(q[:, lo + LANES:lo + 2 * LANES], cos, sin).astype(BF16)
    kvl_ref[:, :kv_lora] = _rms(ckv, kvn_ref[...])
    kvl_ref[:, kv_lora:] = _rope(kpe, cos, sin)


def _mla_proj(rows, xs, mods, layer, norm_w, w_in_p, q_norm, w_uq_p, kv_norm, cos_t, sin_t):
    d = xs[0].shape[1]
    q_lora, kv_lora = q_norm.shape[-1], kv_norm.shape[-1]
    heads = w_uq_p.shape[1] // (2 * LANES)
    split = len(xs) == 2
    kern = functools.partial(_mla_proj_kernel, q_lora=q_lora, kv_lora=kv_lora, heads=heads,
                             split_ptiles=rows.ptiles if split else 0)
    return pl.pallas_call(
        kern,
        grid=(rows.tiles,),
        in_specs=(rows.split_specs(d) if split else [_row_spec(d)]) + [
            _full_spec((1, d)), rows.mod_spec(d, layer, 0), rows.mod_spec(d, layer, 1),
            _full_spec(w_in_p.shape), _full_spec((1, q_lora)), _full_spec(w_uq_p.shape),
            _full_spec((1, kv_lora)), rows.rope_spec(), rows.rope_spec(),
        ],
        out_specs=[_row_spec(heads * 2 * LANES), _row_spec(kv_lora + LANES)],
        out_shape=[jax.ShapeDtypeStruct((rows.n, heads * 2 * LANES), BF16),
                   jax.ShapeDtypeStruct((rows.n, kv_lora + LANES), F32)],
        compiler_params=_cparams("arbitrary"),
        name="mla_proj",
    )(*xs, norm_w.reshape(1, d), mods, mods, w_in_p, q_norm.reshape(1, -1), w_uq_p, kv_norm.reshape(1, -1),
      cos_t, sin_t)


def _mla_attn_kernel(*refs, n_ctx, kv_lora, scale, hp):
    if n_ctx:
        q_ref, cckv_ref, ckpe_ref, kvl_ref, wukv_ref, o_ref, k_s, v_s = refs
    else:
        q_ref, kvl_ref, wukv_ref, o_ref, k_s, v_s = refs

    hw = MLA_NOPE + MLA_V

    @pl.when(pl.program_id(2) == 0)
    def _():
        lat = kvl_ref[...]
        ckv_lat = lat[:, :kv_lora].astype(BF16)
        kpe_lat = lat[:, kv_lora:].astype(BF16)
        for a in range(hp):
            w = wukv_ref[:, a * hw:(a + 1) * hw]
            kv = _dot(ckv_lat, w)
            k_s[a, n_ctx:, :MLA_NOPE] = kv[:, :MLA_NOPE].astype(BF16)
            k_s[a, n_ctx:, MLA_NOPE:] = kpe_lat
            v_s[a, n_ctx:, :] = kv[:, MLA_NOPE:].astype(BF16)
            if n_ctx:
                kvc = _dot(cckv_ref[...].astype(BF16), w)
                k_s[a, :n_ctx, :MLA_NOPE] = kvc[:, :MLA_NOPE].astype(BF16)
                k_s[a, :n_ctx, MLA_NOPE:MLA_NOPE + MLA_ROPE] = ckpe_ref[...].astype(BF16)
                k_s[a, :n_ctx, MLA_NOPE + MLA_ROPE:] = jnp.zeros((n_ctx, LANES - MLA_ROPE), BF16)
                v_s[a, :n_ctx, :] = kvc[:, MLA_NOPE:].astype(BF16)

    qw = 2 * LANES
    ss = [_dot_nt(q_ref[:, a * qw:(a + 1) * qw], k_s[a]) * (scale * LOG2E) for a in range(hp)]
    for a in range(hp):
        s = ss[a]
        m = jnp.max(s, axis=-1, keepdims=True)
        p = jnp.exp2(s - m)
        l = jnp.sum(p, axis=-1, keepdims=True)
        o_ref[:, a * MLA_V:(a + 1) * MLA_V] = (_dot(p.astype(BF16), v_s[a]) / l).astype(BF16)


def _mla_attn(q, kvl, w_ukv_b, row0, nb, seq, heads, ctx=None):
    kv_lora = kvl.shape[1] - LANES
    tq = min(MLA_TQ, seq)
    nq = seq // tq
    assert row0 % seq == 0
    seq0, tile0 = row0 // seq, row0 // tq
    n_ctx = 0 if ctx is None else ctx[0].shape[2]
    lk = n_ctx + seq
    scale = 1.0 / math.sqrt(MLA_NOPE + MLA_ROPE)
    hp = math.gcd(heads, MLA_HEADS_PER_STEP)
    in_specs = [pl.BlockSpec((tq, hp * 2 * LANES), lambda b, h, i: (tile0 + b * nq + i, h))]
    args = [q]
    if ctx is not None:
        cckv, ckpe, j = ctx
        in_specs += [pl.BlockSpec((None, None, n_ctx, kv_lora), lambda b, h, i: (b, j, 0, 0)),
                     pl.BlockSpec((None, None, n_ctx, MLA_ROPE), lambda b, h, i: (b, j, 0, 0))]
        args += [cckv, ckpe]
    in_specs += [pl.BlockSpec((seq, kv_lora + LANES), lambda b, h, i: (seq0 + b, 0)),
                 pl.BlockSpec((kv_lora, hp * (MLA_NOPE + MLA_V)), lambda b, h, i: (0, h))]
    args += [kvl, w_ukv_b]
    kern = functools.partial(_mla_attn_kernel, n_ctx=n_ctx, kv_lora=kv_lora, scale=scale, hp=hp)
    return pl.pallas_call(
        kern,
        grid=(nb, heads // hp, nq),
        in_specs=in_specs,
        out_specs=pl.BlockSpec((tq, hp * MLA_V), lambda b, h, i: (b * nq + i, h)),
        out_shape=jax.ShapeDtypeStruct((nb * seq, heads * MLA_V), BF16),
        scratch_shapes=[pltpu.VMEM((hp, lk, 2 * LANES), BF16), pltpu.VMEM((hp, lk, MLA_V), BF16)],
        compiler_params=_cparams("arbitrary", "arbitrary", "arbitrary"),
        name="mla_attn_ctx" if ctx is None else "mla_attn_lat",
    )(*args)


def _swa_proj_kernel(x_ref, nw_ref, sh_ref, sc_ref, w_ref, cos_ref, sin_ref, q_ref, kv_ref, *, nq, nk):
    x = x_ref[...]
    h = _rms(x, nw_ref[...]) * (1.0 + sc_ref[...]) + sh_ref[...]
    a = _dot(h.astype(BF16), w_ref[...])
    qk = _rope(a[:, :nq + nk], cos_ref[...], sin_ref[...])
    q_ref[...] = qk[:, :nq].astype(BF16)
    kv_ref[:, :nk] = qk[:, nq:]
    kv_ref[:, nk:] = a[:, nq + nk:]


def _swa_proj(rows, x, mods, layer, norm_w, w_qkv_b, nq, nk, cos_t, sin_t):
    d = x.shape[1]
    kern = functools.partial(_swa_proj_kernel, nq=nq, nk=nk)
    return pl.pallas_call(
        kern,
        grid=(rows.tiles,),
        in_specs=[_row_spec(d), _full_spec((1, d)), rows.mod_spec(d, layer, 0), rows.mod_spec(d, layer, 1),
                  _full_spec(w_qkv_b.shape), rows.rope_spec(), rows.rope_spec()],
        out_specs=[_row_spec(nq), _row_spec(2 * nk)],
        out_shape=[jax.ShapeDtypeStruct((rows.n, nq), BF16), jax.ShapeDtypeStruct((rows.n, 2 * nk), F32)],
        compiler_params=_cparams("arbitrary"),
        name="swa_proj",
    )(x, norm_w.reshape(1, d), mods, mods, w_qkv_b, cos_t, sin_t)


def _sink_softmax_pv(s_parts, v_parts, sink):
    m = jnp.maximum(functools.reduce(jnp.maximum, [jnp.max(s, axis=-1, keepdims=True) for s in s_parts]), sink)
    l = jnp.exp2(sink - m)
    o = None
    for s, v in zip(s_parts, v_parts):
        p = jnp.exp2(s - m)
        l = l + jnp.sum(p, axis=-1, keepdims=True)
        pv = _dot(p.astype(BF16), v)
        o = pv if o is None else o + pv
    return o / l


def _stack_heads(q, g, group, dh):
    return jnp.concatenate([q[:, (g * group + t) * dh:(g * group + t + 1) * dh] for t in range(group)], axis=0)


def _sink_column(sink_ref, g, group, rows):
    return jnp.concatenate([jnp.full((rows, 1), sink_ref[g * group + t] * LOG2E, F32) for t in range(group)], axis=0)


def _swa_ctx_kernel(sink_ref, q_ref, kv_ref, o_ref, *, hq, hkv):
    dh = SWA_HEAD_DIM
    group = hq // hkv
    c = LOG2E / math.sqrt(dh)
    rows = q_ref.shape[0]
    q = q_ref[...]
    kv = kv_ref[...].astype(BF16)
    logits = [_dot_nt(_stack_heads(q, g, group, dh), kv[:, g * dh:(g + 1) * dh]) * c for g in range(hkv)]
    for g in range(hkv):
        v = kv[:, (hkv + g) * dh:(hkv + g + 1) * dh]
        o = _sink_softmax_pv([logits[g]], [v], _sink_column(sink_ref, g, group, rows)).astype(BF16)
        for t in range(group):
            hh = g * group + t
            o_ref[:, hh * dh:(hh + 1) * dh] = o[t * rows:(t + 1) * rows, :]


def _swa_ctx(q, kv, sink, nb, seq, hq, hkv):
    kern = functools.partial(_swa_ctx_kernel, hq=hq, hkv=hkv)
    return pl.pallas_call(
        kern,
        grid=(nb,),
        in_specs=[pl.BlockSpec(memory_space=pltpu.SMEM),
                  pl.BlockSpec((seq, hq * SWA_HEAD_DIM), lambda b: (b, 0)),
                  pl.BlockSpec((seq, 2 * hkv * SWA_HEAD_DIM), lambda b: (b, 0))],
        out_specs=pl.BlockSpec((seq, hq * SWA_HEAD_DIM), lambda b: (b, 0)),
        out_shape=jax.ShapeDtypeStruct((nb * seq, hq * SWA_HEAD_DIM), BF16),
        compiler_params=_cparams("arbitrary"),
        name="swa_attn_ctx",
    )(sink, q, kv)


def _swa_lat_kernel(sink_ref, q_ref, kc_ref, vc_ref, kv0_ref, kv1_ref, kv2_ref, o_ref, *, hq, hkv, seq):
    dh = SWA_HEAD_DIM
    group = hq // hkv
    c = LOG2E / math.sqrt(dh)
    blk = SWA_BLOCK
    i = pl.program_id(1)
    qpos = i * blk + lax.broadcasted_iota(I32, (group * blk, 3 * blk), 0) % blk
    kpos = (i - 1) * blk + lax.broadcasted_iota(I32, (group * blk, 3 * blk), 1)
    band = (jnp.abs(qpos - kpos) <= SWA_WINDOW) & (kpos >= 0) & (kpos < seq)
    q = q_ref[...]
    kc = kc_ref[...].astype(BF16)
    vc = vc_ref[...].astype(BF16)
    kvb = jnp.concatenate([kv0_ref[...], kv1_ref[...], kv2_ref[...]], axis=0).astype(BF16)
    logits = []
    for g in range(hkv):
        qg = _stack_heads(q, g, group, dh)
        s_c = _dot_nt(qg, kc[:, g * dh:(g + 1) * dh]) * c
        s_b = jnp.where(band, _dot_nt(qg, kvb[:, g * dh:(g + 1) * dh]) * c, NEG_INF)
        logits.append((s_c, s_b))
    for g in range(hkv):
        v_c = vc[:, g * dh:(g + 1) * dh]
        v_b = kvb[:, (hkv + g) * dh:(hkv + g + 1) * dh]
        o = _sink_softmax_pv(list(logits[g]), [v_c, v_b], _sink_column(sink_ref, g, group, blk)).astype(BF16)
        for t in range(group):
            hh = g * group + t
            o_ref[:, hh * dh:(hh + 1) * dh] = o[t * blk:(t + 1) * blk, :]


def _swa_lat(q, kv, sink, cache_k, cache_v, j, row0, nb, seq, hq, hkv):
    blk = SWA_BLOCK
    nblk = seq // blk
    blk0 = row0 // blk
    n_ctx = cache_k.shape[2]
    wq, wkv = hq * SWA_HEAD_DIM, hkv * SWA_HEAD_DIM
    kern = functools.partial(_swa_lat_kernel, hq=hq, hkv=hkv, seq=seq)

    def band_spec(off):
        return pl.BlockSpec((blk, 2 * wkv), lambda b, i: (blk0 + b * nblk + jnp.clip(i + off, 0, nblk - 1), 0))

    return pl.pallas_call(
        kern,
        grid=(nb, nblk),
        in_specs=[pl.BlockSpec(memory_space=pltpu.SMEM),
                  pl.BlockSpec((blk, wq), lambda b, i: (blk0 + b * nblk + i, 0)),
                  pl.BlockSpec((None, None, n_ctx, wkv), lambda b, i: (b, j, 0, 0)),
                  pl.BlockSpec((None, None, n_ctx, wkv), lambda b, i: (b, j, 0, 0)),
                  band_spec(-1), band_spec(0), band_spec(1)],
        out_specs=pl.BlockSpec((blk, wq), lambda b, i: (b * nblk + i, 0)),
        out_shape=jax.ShapeDtypeStruct((nb * seq, wq), BF16),
        compiler_params=_cparams("arbitrary", "arbitrary"),
        name="swa_attn_lat",
    )(sink, q, cache_k, cache_v, kv, kv, kv)


def _attn_out_kernel(*refs, n_exp, ptiles, tiles_per_seq, split_x):
    o = _pick_rows(refs[0], refs[1], ptiles)
    if split_x:
        x = _pick_rows(refs[2], refs[3], ptiles)
        refs = refs[4:]
    else:
        x = refs[2][...]
        refs = refs[3:]
    wo_ref, gate_ref, nw_ref, sh_ref, sc_ref, wr_ref, wrt_ref, x1_ref, hx_ref, a2_ref = refs
    half = x.shape[1] // 2
    x1 = x + gate_ref[...] * _dot(o, wo_ref[...])
    x1_ref[...] = x1
    h = _rms(x1, nw_ref[...]) * (1.0 + sc_ref[...]) + sh_ref[...]
    hb = h.astype(BF16)
    bits = lax.bitcast_convert_type(hb.astype(F32), U32)
    hx_ref[:, :half] = (bits[:, half:] & U32(HI16)) | (bits[:, :half] >> U32(16))
    lane = lax.broadcasted_iota(I32, (TM, EXT), 1)
    logits = jnp.where(lane < n_exp, _dot(hb, wr_ref[...]), NEG_INF)
    e = jnp.exp(logits - jnp.max(logits, axis=-1, keepdims=True))
    aff = e / jnp.sum(e, axis=-1, keepdims=True)
    i = pl.program_id(0)
    tok = i * TM + lax.broadcasted_iota(I32, (TM, EXT), 0)
    ext = jnp.where(lane == n_exp, (tok // CHUNK).astype(F32), aff)
    ext = jnp.where(lane == n_exp + 1, (tok % CHUNK).astype(F32), ext)
    hx_ref[:, half:] = lax.bitcast_convert_type(ext, U32)
    lt = _dot_nt(wrt_ref[...], hb)
    et = jnp.exp(lt - jnp.max(lt, axis=0, keepdims=True))
    at = et / jnp.sum(et, axis=0, keepdims=True)
    a2_ref[...] = jnp.concatenate([at[:, c * CHUNK:(c + 1) * CHUNK] for c in range(TM // CHUNK)], axis=0)


def _attn_out(rows, o_p, o_s, xs, w_o_b, mods, layer, norm_w, w_router_p, w_router_t):
    d = xs[0].shape[1]
    n_exp = w_router_t.shape[0]
    split_x = len(xs) == 2
    kern = functools.partial(_attn_out_kernel, n_exp=n_exp, ptiles=rows.ptiles, tiles_per_seq=rows.tiles_per_seq,
                             split_x=split_x)
    cpt = TM // CHUNK
    return pl.pallas_call(
        kern,
        grid=(rows.tiles,),
        in_specs=rows.split_specs(o_p.shape[1]) + (rows.split_specs(d) if split_x else [_row_spec(d)]) + [
            _full_spec(w_o_b.shape), rows.mod_spec(d, layer, 2),
            _full_spec((1, d)), rows.mod_spec(d, layer, 3), rows.mod_spec(d, layer, 4),
            _full_spec(w_router_p.shape), _full_spec(w_router_t.shape)],
        out_specs=[_row_spec(d), _row_spec(d // 2 + EXT), pl.BlockSpec((cpt * n_exp, CHUNK), lambda i: (i, 0))],
        out_shape=[jax.ShapeDtypeStruct((rows.n, d), F32), jax.ShapeDtypeStruct((rows.n, d // 2 + EXT), U32),
                   jax.ShapeDtypeStruct((rows.n // CHUNK * n_exp, CHUNK), F32)],
        compiler_params=_cparams("arbitrary"),
        name="attn_out",
    )(o_p, o_s, *xs, w_o_b, mods, norm_w.reshape(1, d), mods, mods, w_router_p, w_router_t)


def _select_group(a3, cap):
    nj, n_exp, _ = a3.shape
    bits = lax.bitcast_convert_type(a3, I32)
    capf = jnp.float32(cap)

    def count(pred):
        c = jnp.sum(pred.astype(F32), axis=0)
        return jnp.sum(c, axis=-1, keepdims=True)[None]

    def thr_step(k, t):
        cand = t | lax.shift_left(jnp.int32(1), 30 - k)
        return jnp.where(count(bits >= cand) >= capf, cand, t)

    thr = lax.fori_loop(0, 31, thr_step, jnp.zeros((1, n_exp, 1), I32))
    gt = bits > thr
    eq = bits == thr
    need = capf - count(gt)
    tok = (lax.broadcasted_iota(I32, a3.shape, 0) * CHUNK + lax.broadcasted_iota(I32, a3.shape, 2))
    nbits = max(1, (nj * CHUNK).bit_length())

    def tie_step(k, x):
        cand = x | lax.shift_left(jnp.int32(1), nbits - 1 - k)
        return jnp.where(count(eq & (tok < cand)) <= need, cand, x)

    bound = lax.fori_loop(0, nbits, tie_step, jnp.zeros((1, n_exp, 1), I32))
    return (gt | (eq & (tok < bound))).astype(F32)


def _compact(mask_e, cap, tri, lstrict):
    cl = _dot(mask_e.astype(BF16), tri)
    cnt = cl[:, CHUNK - 1:CHUNK]
    off = _dot(lstrict, jnp.broadcast_to(cnt, (LANES, LANES)).astype(BF16))[:, :1]
    ends = off + cnt
    r = lax.broadcasted_iota(I32, (1, cap), 1).astype(F32)
    jstar = jnp.sum((ends <= r).astype(F32), axis=0, keepdims=True)
    onehot = lax.broadcasted_iota(I32, (LANES, cap), 0).astype(F32) == jstar
    offsel = jnp.sum(jnp.where(onehot, off, 0.0), axis=0, keepdims=True)
    g = _dot(cl.T.astype(BF16), onehot.astype(BF16))
    local = jnp.sum((g <= r - offsel).astype(F32), axis=0, keepdims=True)
    before = (lax.broadcasted_iota(I32, (LANES, LANES), 0)
              < (TM // CHUNK) * lax.broadcasted_iota(I32, (LANES, LANES), 1))
    tile_off = jnp.sum(jnp.where(before, cnt, 0.0), axis=0, keepdims=True)
    return (jstar * CHUNK + local).astype(I32), tile_off.astype(I32)


def _moe_select_kernel(a2_ref, tri_ref, ls_ref, idx_ref, toff_ref, m_s, *, n_exp, groups):
    tri = tri_ref[...]
    lstrict = ls_ref[...]
    col = 0
    for gi, (chunk0, nj, cap) in enumerate(groups):
        m_s[...] = jnp.zeros(m_s.shape, F32)
        a3 = a2_ref[chunk0 * n_exp:(chunk0 + nj) * n_exp, :].reshape(nj, n_exp, CHUNK)
        m_s[:nj * n_exp, :] = _select_group(a3, cap).reshape(nj * n_exp, CHUNK)
        for e in range(n_exp):
            mask_e = m_s[pl.ds(e, LANES, stride=n_exp), :]
            local_idx, tile_off = _compact(mask_e, cap, tri, lstrict)
            idx_ref[e:e + 1, col:col + cap] = local_idx + chunk0 * CHUNK
            toff_ref[e:e + 1, gi * LANES:(gi + 1) * LANES] = tile_off + col
        col += cap


def _moe_select(a2, n_exp, groups):
    total = sum(g[2] for g in groups)
    assert all(g[1] <= LANES and g[2] % LANES == 0 for g in groups)
    ii = jnp.arange(LANES)
    tri = (ii[:, None] <= ii[None, :]).astype(BF16)
    lstrict = (ii[None, :] < ii[:, None]).astype(BF16)
    kern = functools.partial(_moe_select_kernel, n_exp=n_exp, groups=groups)
    return pl.pallas_call(
        kern,
        grid=(1,),
        in_specs=[_full_spec(a2.shape), _full_spec((LANES, LANES)), _full_spec((LANES, LANES))],
        out_specs=[_full_spec((n_exp, total)), _full_spec((n_exp, len(groups) * LANES))],
        out_shape=[jax.ShapeDtypeStruct((n_exp, total), I32),
                   jax.ShapeDtypeStruct((n_exp, len(groups) * LANES), I32)],
        scratch_shapes=[pltpu.VMEM((LANES * n_exp, CHUNK), F32)],
        compiler_params=_cparams("arbitrary"),
        name="moe_select",
    )(a2, tri, lstrict)


def _moe_ffn_kernel(idx_ref, hx_hbm, wg_ref, wu_ref, wd_ref, y_hbm,
                    xs, acc, ybuf, sem_h, sem_y, *, rows_e, mt, d, n_fc, n_exp_static):
    e = pl.program_id(0)
    fc = pl.program_id(1)
    n_exp = pl.num_programs(0)
    half = d // 2
    slot = e % 2

    def hbm_row(ref, tok):
        return ref.at[tok >> 3, pl.ds(tok & (SUBLANES - 1), 1), :]

    def h_gather(ex, sl):
        def body(r8, c):
            for s in range(SUBLANES):
                tok = idx_ref[ex * rows_e + r8 * SUBLANES + s]
                pltpu.make_async_copy(hbm_row(hx_hbm, tok), xs.at[sl, r8, pl.ds(s, 1), :], sem_h.at[sl]).start()
            return c
        lax.fori_loop(0, rows_e // SUBLANES, body, 0)

    def y_copy(t, b):
        return pltpu.make_async_copy(ybuf.at[b], y_hbm.at[pl.ds(e * rows_e + t * mt, mt), :], sem_y.at[b])

    @pl.when((fc == 0) & (e == 0))
    def _():
        h_gather(0, 0)
        ybuf[0, :Y_PAD, :] = jnp.zeros((Y_PAD, ybuf.shape[2]), BF16)
        pad = pltpu.make_async_copy(ybuf.at[0, pl.ds(0, Y_PAD), :], y_hbm.at[pl.ds(n_exp * rows_e, Y_PAD), :],
                                    sem_y.at[0])
        pad.start()
        pad.wait()

    @pl.when(fc == 0)
    def _():
        pltpu.make_async_copy(hx_hbm.at[pl.ds(0, rows_e // SUBLANES)], xs.at[slot], sem_h.at[slot]).wait()
        acc[...] = jnp.zeros(acc.shape, F32)

    wg = wg_ref[...].astype(BF16)
    wu = wu_ref[...].astype(BF16)
    wd = wd_ref[...].astype(BF16)
    ow = min(2 * LANES, d)
    n_mt = rows_e // mt
    g8 = rows_e // SUBLANES // (n_fc * n_mt)
    nxt = (e + 1) % n_exp
    for t in range(n_mt):
        for k in range(g8):
            r8 = (fc * n_mt + t) * g8 + k
            for s in range(SUBLANES):
                tok = idx_ref[nxt * rows_e + r8 * SUBLANES + s]
                pltpu.make_async_copy(hbm_row(hx_hbm, tok), xs.at[1 - slot, r8, pl.ds(s, 1), :],
                                      sem_h.at[1 - slot]).start()
        w = xs[slot, t * mt // SUBLANES:(t + 1) * mt // SUBLANES, :, :half].reshape(mt, half)
        x = jnp.concatenate([lax.bitcast_convert_type(w << U32(16), F32).astype(BF16),
                             lax.bitcast_convert_type(w & U32(HI16), F32).astype(BF16)], axis=-1)
        gate = _dot(x, wg)
        hid = (gate * jax.nn.sigmoid(gate) * _dot(x, wu)).astype(BF16)
        for c in range(d // ow):
            acc[t * mt:(t + 1) * mt, c * ow:(c + 1) * ow] += _dot(hid, wd[:, c * ow:(c + 1) * ow])

    tail = sorted({(t % 2, t) for t in range(max(n_mt - 2, 0), n_mt)})

    @pl.when((fc == n_fc - 1) & (e > 0))
    def _():
        for b, t in tail:
            y_copy(t, b).wait()

    @pl.when(fc == n_fc - 1)
    def _():
        lane = lax.broadcasted_iota(I32, (mt, EXT), 1)
        for t in range(n_mt):
            b = t % 2
            if t >= 2:
                y_copy(t - 2, b).wait()
            ext = xs[slot, t * mt // SUBLANES:(t + 1) * mt // SUBLANES, :, half:].reshape(mt, EXT)
            ext = lax.bitcast_convert_type(ext, F32)
            g = jnp.sum(jnp.where(lane == e, ext, 0.0), axis=-1, keepdims=True)
            ybuf[b, :, :d] = (acc[t * mt:(t + 1) * mt, :] * g).astype(BF16)
            tag = jnp.where(lane < 2, pltpu.roll(ext, EXT - n_exp_static, 1), 0.0)
            ybuf[b, :, d:] = tag.astype(BF16)
            y_copy(t, b).start()

    @pl.when((fc == n_fc - 1) & (e == n_exp - 1))
    def _():
        for b, t in tail:
            y_copy(t, b).wait()
        pltpu.make_async_copy(hx_hbm.at[pl.ds(0, rows_e // SUBLANES)], xs.at[1 - slot], sem_h.at[1 - slot]).wait()


def _moe_ffn(idx_flat, hx, w_gate, w_up, w_down, layer, rows_e):
    n, wx = hx.shape
    _, n_exp, d, f = w_gate.shape
    fcw = min(256, f)
    n_fc = f // fcw
    mt = math.gcd(rows_e, 2 * TM)
    assert mt % Y_PAD == 0 and (rows_e // SUBLANES) % (n_fc * (rows_e // mt)) == 0
    kern = functools.partial(_moe_ffn_kernel, rows_e=rows_e, mt=mt, d=d, n_fc=n_fc, n_exp_static=n_exp)
    grid_spec = pltpu.PrefetchScalarGridSpec(
        num_scalar_prefetch=1,
        grid=(n_exp, n_fc),
        in_specs=[
            pl.BlockSpec(memory_space=pl.ANY),
            pl.BlockSpec((None, None, d, fcw), lambda e, c, idx: (layer, e, 0, c)),
            pl.BlockSpec((None, None, d, fcw), lambda e, c, idx: (layer, e, 0, c)),
            pl.BlockSpec((None, None, fcw, d), lambda e, c, idx: (layer, e, c, 0)),
        ],
        out_specs=pl.BlockSpec(memory_space=pl.ANY),
        scratch_shapes=[pltpu.VMEM((2, rows_e // SUBLANES, SUBLANES, wx), U32), pltpu.VMEM((rows_e, d), F32),
                        pltpu.VMEM((2, mt, d + EXT), BF16), pltpu.SemaphoreType.DMA((2,)),
                        pltpu.SemaphoreType.DMA((2,))],
    )
    return pl.pallas_call(
        kern,
        grid_spec=grid_spec,
        out_shape=jax.ShapeDtypeStruct((n_exp * rows_e + Y_PAD, d + EXT), BF16),
        compiler_params=_cparams("arbitrary", "arbitrary"),
        name="moe_ffn",
    )(idx_flat, hx.reshape(n // SUBLANES, SUBLANES, wx), w_gate, w_up, w_down)


def _moe_combine_kernel(toff_ref, y_hbm, x1_ref, gate_ref, o_ref, stage, sem, *, n_exp, rows_e, n_tiles, d):
    T = pl.program_id(0)
    n_grp = n_exp // COMBINE_GROUP
    cpk = 2 * LANES // Y_PAD
    cap = stage.shape[1]

    def plan(tq, grp):
        out, total = [], 0
        for k in range(COMBINE_GROUP):
            e = grp * COMBINE_GROUP + k
            lo = toff_ref[e * (n_tiles + 1) + tq]
            hi = toff_ref[e * (n_tiles + 1) + tq + 1]
            c0 = lo // Y_PAD
            cnt = jnp.where(hi > lo, (hi + Y_PAD - 1) // Y_PAD - c0, 0)
            out.append((e * (rows_e // Y_PAD) + c0, cnt, total))
            total = total + cnt
        return out, total

    def issue(tq, grp, b):
        for src, cnt, pos in plan(tq, grp)[0]:
            def body(c, carry):
                pltpu.make_async_copy(y_hbm.at[src + c], stage.at[b, pos + c], sem.at[b]).start()
                return carry
            lax.fori_loop(0, cnt, body, 0)

    def wait(total, b):
        def body(c, carry):
            pltpu.make_async_copy(y_hbm.at[0], stage.at[b, 0], sem.at[b]).wait()
            return carry
        lax.fori_loop(0, total, body, 0)

    @pl.when(T == 0)
    def _():
        stage[...] = jnp.zeros(stage.shape, BF16)
        issue(0, 0, 0)

    o_ref[...] = x1_ref[...]
    gate = gate_ref[...]
    tok = (T * TM + lax.broadcasted_iota(I32, (TM, 2 * LANES), 0)).astype(F32)
    for grp in range(n_grp):
        b = grp % 2
        if grp + 1 < n_grp:
            issue(T, grp + 1, 1 - b)
        else:
            @pl.when(T + 1 < n_tiles)
            def _():
                issue(T + 1, 0, 1 - b)
        total = plan(T, grp)[1]
        wait(total, b)

        def block(kc, carry):
            z = stage[b, pl.ds(kc * cpk, cpk)].reshape(2 * LANES, d + EXT)
            tag = z[:, d:].astype(F32).T
            ztok = tag[0:1, :] * float(CHUNK) + tag[1:2, :]
            zrow = kc * 2 * LANES + lax.broadcasted_iota(I32, (1, 2 * LANES), 1)
            onehot = ((tok == ztok) & (zrow < total * Y_PAD)).astype(BF16)
            o_ref[...] += gate * _dot(onehot, z[:, :d])
            return carry
        lax.fori_loop(0, (total + cpk - 1) // cpk, block, 0)


def _moe_combine(rows, toff_flat, y, x1, mods, layer, n_exp, rows_e):
    n, d = x1.shape
    wy = y.shape[1]
    assert n_exp % COMBINE_GROUP == 0 and (n_exp // COMBINE_GROUP) % 2 == 0 and rows_e % Y_PAD == 0
    per_exp = (TM + 2 * (Y_PAD - 1)) // Y_PAD + 1
    cpk = 2 * LANES // Y_PAD
    cap = -(-COMBINE_GROUP * per_exp // cpk) * cpk
    kern = functools.partial(_moe_combine_kernel, n_exp=n_exp, rows_e=rows_e, n_tiles=rows.tiles, d=d)
    grid_spec = pltpu.PrefetchScalarGridSpec(
        num_scalar_prefetch=1,
        grid=(rows.tiles,),
        in_specs=[pl.BlockSpec(memory_space=pl.ANY),
                  pl.BlockSpec((TM, d), lambda i, t: (i, 0)),
                  pl.BlockSpec((None, 1, d), lambda i, t: ((layer * N_MOD + N_MOD - 1) * 8 + rows.mod_row(i), 0, 0))],
        out_specs=pl.BlockSpec((TM, d), lambda i, t: (i, 0)),
        scratch_shapes=[pltpu.VMEM((2, cap, Y_PAD, wy), BF16), pltpu.SemaphoreType.DMA((2,))],
    )
    return pl.pallas_call(
        kern,
        grid_spec=grid_spec,
        out_shape=jax.ShapeDtypeStruct((n, d), F32),
        compiler_params=_cparams("arbitrary"),
        name="moe_combine",
    )(toff_flat, y.reshape(y.shape[0] // Y_PAD, Y_PAD, wy), x1, mods)


def _final_norm_kernel(x_ref, w_ref, o_ref):
    o_ref[...] = _rms(x_ref[...], w_ref[...])


def _final_norm(x, w, row0, n_rows):
    d = x.shape[1]
    t0 = row0 // TM
    return pl.pallas_call(
        _final_norm_kernel,
        grid=(n_rows // TM,),
        in_specs=[pl.BlockSpec((TM, d), lambda i: (t0 + i, 0)), _full_spec((1, d))],
        out_specs=_row_spec(d),
        out_shape=jax.ShapeDtypeStruct((n_rows, d), F32),
        compiler_params=_cparams("arbitrary"),
        name="final_norm",
    )(x, w.reshape(1, d))


def _rope_tables(seq):
    rows = seq // GRID_W
    row = jnp.repeat(jnp.arange(rows), GRID_W).astype(F32)
    col = jnp.tile(jnp.arange(GRID_W), rows).astype(F32)
    nf = MLA_ROPE // 4
    inv = ROPE_BASE ** (-jnp.arange(nf, dtype=F32) / nf)
    ang = jnp.concatenate([row[:, None] * inv[None], col[:, None] * inv[None]], axis=-1)
    c, s = jnp.cos(ang), jnp.sin(ang)
    cos = jnp.concatenate([c, c, c, c], axis=-1)
    sin = jnp.concatenate([-s, s, -s, s], axis=-1)
    cos = jnp.concatenate([jnp.ones((TM, LANES), F32), cos], axis=0)
    sin = jnp.concatenate([jnp.zeros((TM, LANES), F32), sin], axis=0)
    return cos, sin


def kernel(x_prompt, x_sample, cache_ckv, cache_kpe, cache_k, cache_v, c, c_ctx, ada_w, ada_b, norm_mix, norm_ffn,
           mla_w_in, mla_q_norm, mla_w_uq, mla_kv_norm, mla_w_ukv, mla_w_o, swa_w_qkv, swa_sink, swa_w_o,
           moe_router, moe_w_gate, moe_w_up, moe_w_down, final_norm):
    assert MLA_ROPE == SWA_HEAD_DIM
    batch, seq, d = x_prompt.shape
    dec_batch, dec_seq, _ = x_sample.shape
    depth = ada_w.shape[0]
    n_p, n_s = batch * seq, dec_batch * dec_seq
    rows = _Rows(n_p, n_s, dec_seq)
    assert seq == TM and n_p % dec_seq == 0
    mla_heads = mla_w_uq.shape[-1] // (MLA_NOPE + MLA_ROPE)
    q_lora, kv_lora = mla_q_norm.shape[-1], mla_kv_norm.shape[-1]
    hq, hkv = swa_sink.shape[-1], cache_k.shape[3]
    n_exp = moe_router.shape[-1]
    cap_p, cap_s = EC_CAPACITY * n_p // n_exp, EC_CAPACITY * n_s // n_exp
    groups = ((0, n_p // CHUNK, cap_p), (n_p // CHUNK, n_s // CHUNK, cap_s))

    cv = jnp.zeros((8, d), F32).at[0].set(c_ctx).at[1:1 + dec_batch].set(c)
    mods = _adaln(cv, ada_w, ada_b)
    mods_r = mods.reshape(depth, 8, N_MOD, d).transpose(0, 2, 1, 3)
    mods_rows = mods_r.reshape(depth * N_MOD * 8, 1, d)
    cos_t, sin_t = _rope_tables(dec_seq)

    xs = (x_prompt.reshape(n_p, d), x_sample.reshape(n_s, d))
    new_ckv, new_kpe, new_k, new_v = [], [], [], []
    for i in range(depth):
        j = i // 2
        if i % 2 == 0:
            w_in = mla_w_in[j]
            w_in_p = jnp.pad(w_in, ((0, 0), (0, q_lora + kv_lora + LANES - w_in.shape[1]))).astype(BF16)
            w_uq = mla_w_uq[j].reshape(q_lora, mla_heads, MLA_NOPE + MLA_ROPE)
            w_uq_p = jnp.pad(w_uq, ((0, 0), (0, 0), (0, 2 * LANES - MLA_NOPE - MLA_ROPE)))
            w_uq_p = w_uq_p.reshape(q_lora, mla_heads * 2 * LANES).astype(BF16)
            q, kvl = _mla_proj(rows, xs, mods_rows, i, norm_mix[i], w_in_p, mla_q_norm[j], w_uq_p,
                               mla_kv_norm[j], cos_t, sin_t)
            w_ukv_b = mla_w_ukv[j].astype(BF16)
            o_p = _mla_attn(q, kvl, w_ukv_b, 0, batch, seq, mla_heads)
            o_s = _mla_attn(q, kvl, w_ukv_b, n_p, dec_batch, dec_seq, mla_heads, ctx=(cache_ckv, cache_kpe, j))
            new_ckv.append(kvl[:n_p, :kv_lora].reshape(batch, seq, kv_lora))
            new_kpe.append(kvl[:n_p, kv_lora:kv_lora + MLA_ROPE].reshape(batch, seq, MLA_ROPE))
            w_o_b = mla_w_o[j].astype(BF16)
        else:
            nq, nk = hq * SWA_HEAD_DIM, hkv * SWA_HEAD_DIM
            q, kv = _swa_proj(rows, xs[0], mods_rows, i, norm_mix[i], swa_w_qkv[j].astype(BF16), nq, nk,
                              cos_t, sin_t)
            o_p = _swa_ctx(q, kv, swa_sink[j], batch, seq, hq, hkv)
            ck = cache_k.reshape(cache_k.shape[:3] + (nk,))
            cvv = cache_v.reshape(cache_v.shape[:3] + (nk,))
            o_s = _swa_lat(q, kv, swa_sink[j], ck, cvv, j, n_p, dec_batch, dec_seq, hq, hkv)
            new_k.append(kv[:n_p, :nk].reshape(batch, seq, hkv, SWA_HEAD_DIM))
            new_v.append(kv[:n_p, nk:].reshape(batch, seq, hkv, SWA_HEAD_DIM))
            w_o_b = swa_w_o[j].astype(BF16)
        w_r = moe_router[i]
        w_router_p = jnp.pad(w_r, ((0, 0), (0, EXT - n_exp))).astype(BF16)
        x1, hx, a2 = _attn_out(rows, o_p, o_s, xs, w_o_b, mods_rows, i, norm_ffn[i], w_router_p,
                               w_r.T.astype(BF16))
        idx, toff = _moe_select(a2, n_exp, groups)
        rows_e = cap_p + cap_s
        toff = jnp.concatenate([toff[:, :rows.ptiles], toff[:, LANES:LANES + rows.tiles - rows.ptiles],
                                jnp.full((n_exp, 1), rows_e, I32)], axis=1)
        y = _moe_ffn(idx.reshape(-1), hx, moe_w_gate, moe_w_up, moe_w_down, i, rows_e)
        xs = (_moe_combine(rows, toff.reshape(-1), y, x1, mods_rows, i, n_exp, rows_e),)
    x = xs[0]
    y_prompt = _final_norm(x, final_norm, 0, n_p).reshape(batch, seq, d)
    y_sample = _final_norm(x, final_norm, n_p, n_s).reshape(dec_batch, dec_seq, d)
    return (y_prompt, y_sample, jnp.stack(new_ckv, axis=1), jnp.stack(new_kpe, axis=1),
            jnp.stack(new_k, axis=1), jnp.stack(new_v, axis=1))
```

```python
import functools
import math

import jax
import jax.numpy as jnp
from jax import lax
from jax.experimental import pallas as pl
from jax.experimental.pallas import tpu as pltpu

F32 = jnp.float32
BF16 = jnp.bfloat16
I32 = jnp.int32
U32 = jnp.uint32
HI16 = 0xFFFF0000

GRID_W = 64
RMS_EPS = 1e-6
ROPE_BASE = 10000.0
NEG_INF = -1e30
N_MOD = 6
MLA_NOPE = 128
MLA_ROPE = 64
MLA_V = 128
SWA_HEAD_DIM = 64
SWA_WINDOW = 128
SWA_BLOCK = 128
EC_CAPACITY = 2

LANES = 128
SUBLANES = 8
TM = 256
MLA_TQ = 256
MLA_HEADS_PER_STEP = 4
LOG2E = math.log2(math.e)
CHUNK = 128
EXT = 128
Y_PAD = 16
COMBINE_GROUP = 4
VMEM_LIMIT = 52 * 1024 * 1024
PROJ_TILE = 512
ROPE_IDENT = 512
FFN_MT = 768
FFN_VMEM_LIMIT = 58 * 1024 * 1024


def _cparams(*sem, vmem=VMEM_LIMIT):
    return pltpu.CompilerParams(dimension_semantics=sem, vmem_limit_bytes=vmem)


def _rms(x, w):
    return x * lax.rsqrt(jnp.mean(x * x, axis=-1, keepdims=True) + RMS_EPS) * w


def _swap_halves(x):
    w = x.shape[-1]
    lane = lax.broadcasted_iota(I32, x.shape, x.ndim - 1)
    return jnp.where(lane % 64 < 32, pltpu.roll(x, w - 32, x.ndim - 1), pltpu.roll(x, 32, x.ndim - 1))


def _rope(x, cos, sin):
    n = x.shape[-1] // LANES
    cw = jnp.concatenate([cos] * n, axis=-1) if n > 1 else cos
    sw = jnp.concatenate([sin] * n, axis=-1) if n > 1 else sin
    return x * cw + _swap_halves(x) * sw


def _ones_column(rows, width):
    return jnp.where(lax.broadcasted_iota(I32, (rows, width), 1) == 0, 1.0, 0.0).astype(BF16)


def _dot(a, b):
    return jnp.dot(a, b, preferred_element_type=F32)


def _dot_nt(a, b):
    return lax.dot_general(a, b, (((1,), (1,)), ((), ())), preferred_element_type=F32)


def _adaln_kernel(cv_ref, w_ref, b_ref, o_ref):
    a = cv_ref[...]
    a = a * jax.nn.sigmoid(a)
    o_ref[...] = _dot(a.astype(BF16), w_ref[...].astype(BF16)) + b_ref[...]


def _adaln(cv, ada_w, ada_b):
    depth, d, n6 = ada_w.shape
    tn = d // 2
    return pl.pallas_call(
        _adaln_kernel,
        grid=(depth, n6 // tn),
        in_specs=[
            pl.BlockSpec((8, d), lambda l, j: (0, 0)),
            pl.BlockSpec((None, d, tn), lambda l, j: (l, 0, j)),
            pl.BlockSpec((None, 1, tn), lambda l, j: (l, 0, j)),
        ],
        out_specs=pl.BlockSpec((None, 8, tn), lambda l, j: (l, 0, j)),
        out_shape=jax.ShapeDtypeStruct((depth, 8, n6), F32),
        compiler_params=_cparams("arbitrary", "arbitrary"),
        name="adaln",
    )(cv, ada_w, ada_b.reshape(depth, 1, n6))


class _Rows:
    def __init__(self, n_prompt, n_sample, dec_seq, tile=TM):
        self.n_prompt, self.n_sample, self.dec_seq, self.tile = n_prompt, n_sample, dec_seq, tile
        self.n = n_prompt + n_sample
        assert n_prompt % tile == 0 and dec_seq % tile == 0 and ROPE_IDENT % tile == 0
        self.tiles = self.n // tile
        self.ptiles = n_prompt // tile
        self.tiles_per_seq = dec_seq // tile

    def mod_row(self, i):
        return jnp.where(i < self.ptiles, 0, 1 + (i - self.ptiles) // self.tiles_per_seq)

    def rope_block(self, i):
        return jnp.where(i < self.ptiles, 0, ROPE_IDENT // self.tile + (i - self.ptiles) % self.tiles_per_seq)

    def mod_spec(self, d, layer, k):
        base = (layer * N_MOD + k) * 8
        return pl.BlockSpec((None, 1, d), lambda i: (base + self.mod_row(i), 0, 0))

    def rope_spec(self):
        return pl.BlockSpec((self.tile, LANES), lambda i: (self.rope_block(i), 0))

    def row_spec(self, width):
        return pl.BlockSpec((self.tile, width), lambda i: (i, 0))

    def split_specs(self, width):
        return [pl.BlockSpec((self.tile, width), lambda i: (jnp.minimum(i, self.ptiles - 1), 0)),
                pl.BlockSpec((self.tile, width), lambda i: (jnp.maximum(i - self.ptiles, 0), 0))]


def _pick_rows(p_ref, s_ref, ptiles):
    return jnp.where(pl.program_id(0) < ptiles, p_ref[...], s_ref[...])


def _row_spec(width):
    return pl.BlockSpec((TM, width), lambda i: (i, 0))


def _full_spec(shape):
    nd = len(shape)
    return pl.BlockSpec(shape, lambda i: (0,) * nd)


def _mla_proj_kernel(*refs, q_lora, kv_lora, heads, split_ptiles):
    if split_ptiles:
        x = _pick_rows(refs[0], refs[1], split_ptiles)
        refs = refs[2:]
    else:
        x = refs[0][...]
        refs = refs[1:]
    nw_ref, sh_ref, sc_ref, win_ref, qn_ref, wuq_ref, kvn_ref, cos_ref, sin_ref, q_ref, kvl_ref = refs
    h = _rms(x, nw_ref[...]) * (1.0 + sc_ref[...]) + sh_ref[...]
    a = _dot(h.astype(BF16), win_ref[...])
    cq = a[:, :q_lora]
    ckv = a[:, q_lora:q_lora + kv_lora]
    kpe = a[:, q_lora + kv_lora:q_lora + kv_lora + LANES]
    cos, sin = cos_ref[...], sin_ref[...]
    q = _dot(_rms(cq, qn_ref[...]).astype(BF16), wuq_ref[...])
    for hh in range(heads):
        lo = hh * 2 * LANES
        q_ref[:, lo:lo + LANES] = q[:, lo:lo + LANES].astype(BF16)
        q_ref[:, lo + LANES:lo + 2 * LANES] = _rope(q[:, lo + LANES:lo + 2 * LANES], cos, sin).astype(BF16)
    kvl_ref[:, :kv_lora] = _rms(ckv, kvn_ref[...])
    kvl_ref[:, kv_lora:] = _rope(kpe, cos, sin)


def _mla_proj(rows, xs, mods, layer, norm_w, w_in_p, q_norm, w_uq_p, kv_norm, cos_t, sin_t):
    d = xs[0].shape[1]
    q_lora, kv_lora = q_norm.shape[-1], kv_norm.shape[-1]
    heads = w_uq_p.shape[1] // (2 * LANES)
    split = len(xs) == 2
    kern = functools.partial(_mla_proj_kernel, q_lora=q_lora, kv_lora=kv_lora, heads=heads,
                             split_ptiles=rows.ptiles if split else 0)
    return pl.pallas_call(
        kern,
        grid=(rows.tiles,),
        in_specs=(rows.split_specs(d) if split else [rows.row_spec(d)]) + [
            _full_spec((1, d)), rows.mod_spec(d, layer, 0), rows.mod_spec(d, layer, 1),
            _full_spec(w_in_p.shape), _full_spec((1, q_lora)), _full_spec(w_uq_p.shape),
            _full_spec((1, kv_lora)), rows.rope_spec(), rows.rope_spec(),
        ],
        out_specs=[rows.row_spec(heads * 2 * LANES), rows.row_spec(kv_lora + LANES)],
        out_shape=[jax.ShapeDtypeStruct((rows.n, heads * 2 * LANES), BF16),
                   jax.ShapeDtypeStruct((rows.n, kv_lora + LANES), F32)],
        compiler_params=_cparams("arbitrary"),
        name="mla_proj",
    )(*xs, norm_w.reshape(1, d), mods, mods, w_in_p, q_norm.reshape(1, -1), w_uq_p, kv_norm.reshape(1, -1),
      cos_t, sin_t)


def _mla_attn_kernel(*refs, n_ctx, kv_lora, scale, hp):
    if n_ctx:
        q_ref, cckv_ref, ckpe_ref, kvl_ref, wukv_ref, o_ref, k_s, v_s = refs
    else:
        q_ref, kvl_ref, wukv_ref, o_ref, k_s, v_s = refs

    hw = MLA_NOPE + MLA_V

    @pl.when(pl.program_id(2) == 0)
    def _():
        lat = kvl_ref[...]
        ckv_lat = lat[:, :kv_lora].astype(BF16)
        kpe_lat = lat[:, kv_lora:].astype(BF16)
        for a in range(hp):
            w = wukv_ref[:, a * hw:(a + 1) * hw]
            kv = _dot(ckv_lat, w)
            k_s[a, n_ctx:, :MLA_NOPE] = kv[:, :MLA_NOPE].astype(BF16)
            k_s[a, n_ctx:, MLA_NOPE:] = kpe_lat
            v_s[a, n_ctx:, :MLA_V] = kv[:, MLA_NOPE:].astype(BF16)
            v_s[a, :, MLA_V:] = _ones_column(v_s.shape[1], LANES)
            if n_ctx:
                kvc = _dot(cckv_ref[...].astype(BF16), w)
                k_s[a, :n_ctx, :MLA_NOPE] = kvc[:, :MLA_NOPE].astype(BF16)
                k_s[a, :n_ctx, MLA_NOPE:MLA_NOPE + MLA_ROPE] = ckpe_ref[...].astype(BF16)
                k_s[a, :n_ctx, MLA_NOPE + MLA_ROPE:] = jnp.zeros((n_ctx, LANES - MLA_ROPE), BF16)
                v_s[a, :n_ctx, :MLA_V] = kvc[:, MLA_NOPE:].astype(BF16)

    qw = 2 * LANES
    ss = [_dot_nt(q_ref[:, a * qw:(a + 1) * qw], k_s[a]) * (scale * LOG2E) for a in range(hp)]
    for a in range(hp):
        s = ss[a]
        m = jnp.max(s, axis=-1, keepdims=True)
        p = jnp.exp2((s - m).astype(BF16))
        ov = _dot(p, v_s[a])
        o_ref[:, a * MLA_V:(a + 1) * MLA_V] = (ov[:, :MLA_V] / ov[:, MLA_V:MLA_V + 1]).astype(BF16)


def _mla_attn(q, kvl, w_ukv_b, row0, nb, seq, heads, ctx=None):
    kv_lora = kvl.shape[1] - LANES
    tq = min(MLA_TQ, seq)
    nq = seq // tq
    assert row0 % seq == 0
    seq0, tile0 = row0 // seq, row0 // tq
    n_ctx = 0 if ctx is None else ctx[0].shape[2]
    lk = n_ctx + seq
    scale = 1.0 / math.sqrt(MLA_NOPE + MLA_ROPE)
    hp = math.gcd(heads, MLA_HEADS_PER_STEP)
    in_specs = [pl.BlockSpec((tq, hp * 2 * LANES), lambda b, h, i: (tile0 + b * nq + i, h))]
    args = [q]
    if ctx is not None:
        cckv, ckpe, j = ctx
        in_specs += [pl.BlockSpec((None, None, n_ctx, kv_lora), lambda b, h, i: (b, j, 0, 0)),
                     pl.BlockSpec((None, None, n_ctx, MLA_ROPE), lambda b, h, i: (b, j, 0, 0))]
        args += [cckv, ckpe]
    in_specs += [pl.BlockSpec((seq, kv_lora + LANES), lambda b, h, i: (seq0 + b, 0)),
                 pl.BlockSpec((kv_lora, hp * (MLA_NOPE + MLA_V)), lambda b, h, i: (0, h))]
    args += [kvl, w_ukv_b]
    kern = functools.partial(_mla_attn_kernel, n_ctx=n_ctx, kv_lora=kv_lora, scale=scale, hp=hp)
    return pl.pallas_call(
        kern,
        grid=(nb, heads // hp, nq),
        in_specs=in_specs,
        out_specs=pl.BlockSpec((tq, hp * MLA_V), lambda b, h, i: (b * nq + i, h)),
        out_shape=jax.ShapeDtypeStruct((nb * seq, heads * MLA_V), BF16),
        scratch_shapes=[pltpu.VMEM((hp, lk, 2 * LANES), BF16), pltpu.VMEM((hp, lk, MLA_V + LANES), BF16)],
        compiler_params=_cparams("arbitrary", "arbitrary", "arbitrary"),
        name="mla_attn_ctx" if ctx is None else "mla_attn_lat",
    )(*args)


def _swa_proj_kernel(x_ref, nw_ref, sh_ref, sc_ref, w_ref, cos_ref, sin_ref, q_ref, kv_ref, *, nq, nk):
    x = x_ref[...]
    h = _rms(x, nw_ref[...]) * (1.0 + sc_ref[...]) + sh_ref[...]
    a = _dot(h.astype(BF16), w_ref[...])
    qk = _rope(a[:, :nq + nk], cos_ref[...], sin_ref[...])
    q_ref[...] = qk[:, :nq].astype(BF16)
    kv_ref[:, :nk] = qk[:, nq:]
    kv_ref[:, nk:] = a[:, nq + nk:]


def _swa_proj(rows, x, mods, layer, norm_w, w_qkv_b, nq, nk, cos_t, sin_t):
    d = x.shape[1]
    kern = functools.partial(_swa_proj_kernel, nq=nq, nk=nk)
    return pl.pallas_call(
        kern,
        grid=(rows.tiles,),
        in_specs=[rows.row_spec(d), _full_spec((1, d)), rows.mod_spec(d, layer, 0), rows.mod_spec(d, layer, 1),
                  _full_spec(w_qkv_b.shape), rows.rope_spec(), rows.rope_spec()],
        out_specs=[rows.row_spec(nq), rows.row_spec(2 * nk)],
        out_shape=[jax.ShapeDtypeStruct((rows.n, nq), BF16), jax.ShapeDtypeStruct((rows.n, 2 * nk), F32)],
        compiler_params=_cparams("arbitrary"),
        name="swa_proj",
    )(x, norm_w.reshape(1, d), mods, mods, w_qkv_b, cos_t, sin_t)


def _sink_softmax_pv(s_parts, v_parts, sink, mxu_sum):
    m = jnp.maximum(functools.reduce(jnp.maximum, [jnp.max(s, axis=-1, keepdims=True) for s in s_parts]), sink)
    if mxu_sum:
        dv = v_parts[0].shape[1]
        o = None
        for s, v in zip(s_parts, v_parts):
            pv = _dot(jnp.exp2((s - m).astype(BF16)), jnp.concatenate([v, _ones_column(v.shape[0], dv)], axis=1))
            o = pv if o is None else o + pv
        return o[:, :dv] / (o[:, dv:dv + 1] + jnp.exp2(sink - m))
    l = jnp.exp2(sink - m)
    o = None
    for s, v in zip(s_parts, v_parts):
        p = jnp.exp2(s - m)
        l = l + jnp.sum(p, axis=-1, keepdims=True)
        pv = _dot(p.astype(BF16), v)
        o = pv if o is None else o + pv
    return o / l


def _stack_heads(q, g, group, dh):
    return jnp.concatenate([q[:, (g * group + t) * dh:(g * group + t + 1) * dh] for t in range(group)], axis=0)


def _sink_column(sink_ref, g, group, rows):
    return jnp.concatenate([jnp.full((rows, 1), sink_ref[g * group + t] * LOG2E, F32) for t in range(group)], axis=0)


def _swa_ctx_kernel(sink_ref, q_ref, kv_ref, o_ref, *, hq, hkv):
    dh = SWA_HEAD_DIM
    group = hq // hkv
    c = LOG2E / math.sqrt(dh)
    rows = q_ref.shape[0]
    q = q_ref[...]
    kv = kv_ref[...].astype(BF16)
    logits = [_dot_nt(_stack_heads(q, g, group, dh), kv[:, g * dh:(g + 1) * dh]) * c for g in range(hkv)]
    for g in range(hkv):
        v = kv[:, (hkv + g) * dh:(hkv + g + 1) * dh]
        o = _sink_softmax_pv([logits[g]], [v], _sink_column(sink_ref, g, group, rows), True).astype(BF16)
        for t in range(group):
            hh = g * group + t
            o_ref[:, hh * dh:(hh + 1) * dh] = o[t * rows:(t + 1) * rows, :]


def _swa_ctx(q, kv, sink, nb, seq, hq, hkv):
    kern = functools.partial(_swa_ctx_kernel, hq=hq, hkv=hkv)
    return pl.pallas_call(
        kern,
        grid=(nb,),
        in_specs=[pl.BlockSpec(memory_space=pltpu.SMEM),
                  pl.BlockSpec((seq, hq * SWA_HEAD_DIM), lambda b: (b, 0)),
                  pl.BlockSpec((seq, 2 * hkv * SWA_HEAD_DIM), lambda b: (b, 0))],
        out_specs=pl.BlockSpec((seq, hq * SWA_HEAD_DIM), lambda b: (b, 0)),
        out_shape=jax.ShapeDtypeStruct((nb * seq, hq * SWA_HEAD_DIM), BF16),
        compiler_params=_cparams("arbitrary"),
        name="swa_attn_ctx",
    )(sink, q, kv)


def _swa_lat_kernel(sink_ref, q_ref, kc_ref, vc_ref, kv0_ref, kv1_ref, kv2_ref, o_ref, *, hq, hkv, seq):
    dh = SWA_HEAD_DIM
    group = hq // hkv
    c = LOG2E / math.sqrt(dh)
    blk = SWA_BLOCK
    i = pl.program_id(1)
    qpos = i * blk + lax.broadcasted_iota(I32, (group * blk, 3 * blk), 0) % blk
    kpos = (i - 1) * blk + lax.broadcasted_iota(I32, (group * blk, 3 * blk), 1)
    band = (jnp.abs(qpos - kpos) <= SWA_WINDOW) & (kpos >= 0) & (kpos < seq)
    q = q_ref[...]
    kc = kc_ref[...].astype(BF16)
    vc = vc_ref[...].astype(BF16)
    kvb = jnp.concatenate([kv0_ref[...], kv1_ref[...], kv2_ref[...]], axis=0).astype(BF16)
    logits = []
    for g in range(hkv):
        qg = _stack_heads(q, g, group, dh)
        s_c = _dot_nt(qg, kc[:, g * dh:(g + 1) * dh]) * c
        s_b = jnp.where(band, _dot_nt(qg, kvb[:, g * dh:(g + 1) * dh]) * c, NEG_INF)
        logits.append((s_c, s_b))
    for g in range(hkv):
        v_c = vc[:, g * dh:(g + 1) * dh]
        v_b = kvb[:, (hkv + g) * dh:(hkv + g + 1) * dh]
        o = _sink_softmax_pv(list(logits[g]), [v_c, v_b], _sink_column(sink_ref, g, group, blk), False).astype(BF16)
        for t in range(group):
            hh = g * group + t
            o_ref[:, hh * dh:(hh + 1) * dh] = o[t * blk:(t + 1) * blk, :]


def _swa_lat(q, kv, sink, cache_k, cache_v, j, row0, nb, seq, hq, hkv):
    blk = SWA_BLOCK
    nblk = seq // blk
    blk0 = row0 // blk
    n_ctx = cache_k.shape[2]
    wq, wkv = hq * SWA_HEAD_DIM, hkv * SWA_HEAD_DIM
    kern = functools.partial(_swa_lat_kernel, hq=hq, hkv=hkv, seq=seq)

    def band_spec(off):
        return pl.BlockSpec((blk, 2 * wkv), lambda b, i: (blk0 + b * nblk + jnp.clip(i + off, 0, nblk - 1), 0))

    return pl.pallas_call(
        kern,
        grid=(nb, nblk),
        in_specs=[pl.BlockSpec(memory_space=pltpu.SMEM),
                  pl.BlockSpec((blk, wq), lambda b, i: (blk0 + b * nblk + i, 0)),
                  pl.BlockSpec((None, None, n_ctx, wkv), lambda b, i: (b, j, 0, 0)),
                  pl.BlockSpec((None, None, n_ctx, wkv), lambda b, i: (b, j, 0, 0)),
                  band_spec(-1), band_spec(0), band_spec(1)],
        out_specs=pl.BlockSpec((blk, wq), lambda b, i: (b * nblk + i, 0)),
        out_shape=jax.ShapeDtypeStruct((nb * seq, wq), BF16),
        compiler_params=_cparams("arbitrary", "arbitrary"),
        name="swa_attn_lat",
    )(sink, q, cache_k, cache_v, kv, kv, kv)


def _attn_out_kernel(*refs, n_exp, ptiles, split_x):
    o = _pick_rows(refs[0], refs[1], ptiles)
    tm = o.shape[0]
    if split_x:
        x = _pick_rows(refs[2], refs[3], ptiles)
        refs = refs[4:]
    else:
        x = refs[2][...]
        refs = refs[3:]
    wo_ref, gate_ref, nw_ref, sh_ref, sc_ref, wr_ref, wrt_ref, x1_ref, hx_ref, a2_ref = refs
    half = x.shape[1] // 2
    x1 = x + gate_ref[...] * _dot(o, wo_ref[...])
    x1_ref[...] = x1
    h = _rms(x1, nw_ref[...]) * (1.0 + sc_ref[...]) + sh_ref[...]
    hb = h.astype(BF16)
    bits = lax.bitcast_convert_type(hb.astype(F32), U32)
    hx_ref[:, :half] = (bits[:, half:] & U32(HI16)) | (bits[:, :half] >> U32(16))
    lane = lax.broadcasted_iota(I32, (tm, EXT), 1)
    logits = jnp.where(lane < n_exp, _dot(hb, wr_ref[...]), NEG_INF)
    e = jnp.exp(logits - jnp.max(logits, axis=-1, keepdims=True))
    aff = e / jnp.sum(e, axis=-1, keepdims=True)
    i = pl.program_id(0)
    tok = i * tm + lax.broadcasted_iota(I32, (tm, EXT), 0)
    ext = jnp.where(lane == n_exp, (tok // CHUNK).astype(F32), aff)
    ext = jnp.where(lane == n_exp + 1, (tok % CHUNK).astype(F32), ext)
    hx_ref[:, half:] = lax.bitcast_convert_type(ext, U32)
    lt = _dot_nt(wrt_ref[...], hb)
    et = jnp.exp(lt - jnp.max(lt, axis=0, keepdims=True))
    at = et / jnp.sum(et, axis=0, keepdims=True)
    a2_ref[...] = jnp.concatenate([at[:, c * CHUNK:(c + 1) * CHUNK] for c in range(tm // CHUNK)], axis=0)


def _attn_out(rows, o_p, o_s, xs, w_o_b, mods, layer, norm_w, w_router_p, w_router_t):
    d = xs[0].shape[1]
    n_exp = w_router_t.shape[0]
    split_x = len(xs) == 2
    kern = functools.partial(_attn_out_kernel, n_exp=n_exp, ptiles=rows.ptiles, split_x=split_x)
    cpt = rows.tile // CHUNK
    return pl.pallas_call(
        kern,
        grid=(rows.tiles,),
        in_specs=rows.split_specs(o_p.shape[1]) + (rows.split_specs(d) if split_x else [rows.row_spec(d)]) + [
            _full_spec(w_o_b.shape), rows.mod_spec(d, layer, 2),
            _full_spec((1, d)), rows.mod_spec(d, layer, 3), rows.mod_spec(d, layer, 4),
            _full_spec(w_router_p.shape), _full_spec(w_router_t.shape)],
        out_specs=[rows.row_spec(d), rows.row_spec(d // 2 + EXT),
                   pl.BlockSpec((cpt * n_exp, CHUNK), lambda i: (i, 0))],
        out_shape=[jax.ShapeDtypeStruct((rows.n, d), F32), jax.ShapeDtypeStruct((rows.n, d // 2 + EXT), U32),
                   jax.ShapeDtypeStruct((rows.n // CHUNK * n_exp, CHUNK), F32)],
        compiler_params=_cparams("arbitrary"),
        name="attn_out",
    )(o_p, o_s, *xs, w_o_b, mods, norm_w.reshape(1, d), mods, mods, w_router_p, w_router_t)


def _select_group(a3, cap):
    nj, n_exp, _ = a3.shape
    bits = lax.bitcast_convert_type(a3, I32)
    capf = jnp.float32(cap)

    def count(pred):
        c = jnp.sum(pred.astype(F32), axis=0)
        return jnp.sum(c, axis=-1, keepdims=True)[None]

    def thr_step(k, t):
        cand = t | lax.shift_left(jnp.int32(1), 30 - k)
        return jnp.where(count(bits >= cand) >= capf, cand, t)

    thr = lax.fori_loop(0, 31, thr_step, jnp.zeros((1, n_exp, 1), I32))
    gt = bits > thr
    eq = bits == thr
    need = capf - count(gt)
    tok = (lax.broadcasted_iota(I32, a3.shape, 0) * CHUNK + lax.broadcasted_iota(I32, a3.shape, 2))
    nbits = max(1, (nj * CHUNK).bit_length())

    def tie_step(k, x):
        cand = x | lax.shift_left(jnp.int32(1), nbits - 1 - k)
        return jnp.where(count(eq & (tok < cand)) <= need, cand, x)

    bound = lax.fori_loop(0, nbits, tie_step, jnp.zeros((1, n_exp, 1), I32))
    return (gt | (eq & (tok < bound))).astype(F32)


def _compact(mask_e, cap, tri, lstrict):
    cl = _dot(mask_e.astype(BF16), tri)
    cnt = cl[:, CHUNK - 1:CHUNK]
    off = _dot(lstrict, jnp.broadcast_to(cnt, (LANES, LANES)).astype(BF16))[:, :1]
    ends = off + cnt
    r = lax.broadcasted_iota(I32, (1, cap), 1).astype(F32)
    jstar = jnp.sum((ends <= r).astype(F32), axis=0, keepdims=True)
    onehot = lax.broadcasted_iota(I32, (LANES, cap), 0).astype(F32) == jstar
    offsel = jnp.sum(jnp.where(onehot, off, 0.0), axis=0, keepdims=True)
    g = _dot(cl.T.astype(BF16), onehot.astype(BF16))
    local = jnp.sum((g <= r - offsel).astype(F32), axis=0, keepdims=True)
    before = (lax.broadcasted_iota(I32, (LANES, LANES), 0)
              < (TM // CHUNK) * lax.broadcasted_iota(I32, (LANES, LANES), 1))
    tile_off = jnp.sum(jnp.where(before, cnt, 0.0), axis=0, keepdims=True)
    return (jstar * CHUNK + local).astype(I32), tile_off.astype(I32)


def _moe_select_kernel(a2_ref, tri_ref, ls_ref, idx_ref, toff_ref, m_s, *, n_exp, groups):
    tri = tri_ref[...]
    lstrict = ls_ref[...]
    col = 0
    for gi, (chunk0, nj, cap) in enumerate(groups):
        m_s[...] = jnp.zeros(m_s.shape, F32)
        a3 = a2_ref[chunk0 * n_exp:(chunk0 + nj) * n_exp, :].reshape(nj, n_exp, CHUNK)
        m_s[:nj * n_exp, :] = _select_group(a3, cap).reshape(nj * n_exp, CHUNK)
        for e in range(n_exp):
            mask_e = m_s[pl.ds(e, LANES, stride=n_exp), :]
            local_idx, tile_off = _compact(mask_e, cap, tri, lstrict)
            idx_ref[e:e + 1, col:col + cap] = local_idx + chunk0 * CHUNK
            toff_ref[e:e + 1, gi * LANES:(gi + 1) * LANES] = tile_off + col
        col += cap


def _moe_select(a2, n_exp, groups):
    total = sum(g[2] for g in groups)
    assert all(g[1] <= LANES and g[2] % LANES == 0 for g in groups)
    ii = jnp.arange(LANES)
    tri = (ii[:, None] <= ii[None, :]).astype(BF16)
    lstrict = (ii[None, :] < ii[:, None]).astype(BF16)
    kern = functools.partial(_moe_select_kernel, n_exp=n_exp, groups=groups)
    return pl.pallas_call(
        kern,
        grid=(1,),
        in_specs=[_full_spec(a2.shape), _full_spec((LANES, LANES)), _full_spec((LANES, LANES))],
        out_specs=[_full_spec((n_exp, total)), _full_spec((n_exp, len(groups) * LANES))],
        out_shape=[jax.ShapeDtypeStruct((n_exp, total), I32),
                   jax.ShapeDtypeStruct((n_exp, len(groups) * LANES), I32)],
        scratch_shapes=[pltpu.VMEM((LANES * n_exp, CHUNK), F32)],
        compiler_params=_cparams("arbitrary"),
        name="moe_select",
    )(a2, tri, lstrict)


def _moe_ffn_kernel(idx_ref, hx_hbm, wg_ref, wu_ref, wd_ref, y_hbm,
                    xs, acc, ybuf, sem_h, sem_y, *, rows_e, mt, d, n_fc, n_exp_static):
    e = pl.program_id(0)
    fc = pl.program_id(1)
    n_exp = pl.num_programs(0)
    half = d // 2
    slot = e % 2

    def hbm_row(ref, tok):
        return ref.at[tok >> 3, pl.ds(tok & (SUBLANES - 1), 1), :]

    def h_gather(ex, sl):
        def body(r8, c):
            for s in range(SUBLANES):
                tok = idx_ref[ex * rows_e + r8 * SUBLANES + s]
                pltpu.make_async_copy(hbm_row(hx_hbm, tok), xs.at[sl, r8, pl.ds(s, 1), :], sem_h.at[sl]).start()
            return c
        lax.fori_loop(0, rows_e // SUBLANES, body, 0)

    yt = ybuf.shape[1]
    n_yt = rows_e // yt

    def y_copy(t, b):
        return pltpu.make_async_copy(ybuf.at[b], y_hbm.at[pl.ds(e * rows_e + t * yt, yt), :], sem_y.at[b])

    @pl.when((fc == 0) & (e == 0))
    def _():
        h_gather(0, 0)
        ybuf[0, :Y_PAD, :] = jnp.zeros((Y_PAD, ybuf.shape[2]), BF16)
        pad = pltpu.make_async_copy(ybuf.at[0, pl.ds(0, Y_PAD), :], y_hbm.at[pl.ds(n_exp * rows_e, Y_PAD), :],
                                    sem_y.at[0])
        pad.start()
        pad.wait()

    @pl.when(fc == 0)
    def _():
        pltpu.make_async_copy(hx_hbm.at[pl.ds(0, rows_e // SUBLANES)], xs.at[slot], sem_h.at[slot]).wait()
        acc[...] = jnp.zeros(acc.shape, F32)

    wg = wg_ref[...].astype(BF16)
    wu = wu_ref[...].astype(BF16)
    wd = wd_ref[...].astype(BF16)
    ow = min(2 * LANES, d)
    n_mt = rows_e // mt
    g8 = rows_e // SUBLANES // (n_fc * n_mt)
    nxt = (e + 1) % n_exp
    for t in range(n_mt):
        for k in range(g8):
            r8 = (fc * n_mt + t) * g8 + k
            for s in range(SUBLANES):
                tok = idx_ref[nxt * rows_e + r8 * SUBLANES + s]
                pltpu.make_async_copy(hbm_row(hx_hbm, tok), xs.at[1 - slot, r8, pl.ds(s, 1), :],
                                      sem_h.at[1 - slot]).start()
        w = xs[slot, t * mt // SUBLANES:(t + 1) * mt // SUBLANES, :, :half].reshape(mt, half)
        x = jnp.concatenate([lax.bitcast_convert_type(w << U32(16), F32).astype(BF16),
                             lax.bitcast_convert_type(w & U32(HI16), F32).astype(BF16)], axis=-1)
        gate = _dot(x, wg)
        hid = (gate * jax.nn.sigmoid(gate) * _dot(x, wu)).astype(BF16)
        for c in range(d // ow):
            acc[t * mt:(t + 1) * mt, c * ow:(c + 1) * ow] += _dot(hid, wd[:, c * ow:(c + 1) * ow])

    tail = sorted({(t % 2, t) for t in range(max(n_yt - 2, 0), n_yt)})

    @pl.when((fc == n_fc - 1) & (e > 0))
    def _():
        for b, t in tail:
            y_copy(t, b).wait()

    @pl.when(fc == n_fc - 1)
    def _():
        lane = lax.broadcasted_iota(I32, (yt, EXT), 1)
        for t in range(n_yt):
            b = t % 2
            if t >= 2:
                y_copy(t - 2, b).wait()
            ext = xs[slot, t * yt // SUBLANES:(t + 1) * yt // SUBLANES, :, half:].reshape(yt, EXT)
            ext = lax.bitcast_convert_type(ext, F32)
            g = jnp.sum(jnp.where(lane == e, ext, 0.0), axis=-1, keepdims=True)
            ybuf[b, :, :d] = (acc[t * yt:(t + 1) * yt, :] * g).astype(BF16)
            tag = jnp.where(lane < 2, pltpu.roll(ext, EXT - n_exp_static, 1), 0.0)
            ybuf[b, :, d:] = tag.astype(BF16)
            y_copy(t, b).start()

    @pl.when((fc == n_fc - 1) & (e == n_exp - 1))
    def _():
        for b, t in tail:
            y_copy(t, b).wait()
        pltpu.make_async_copy(hx_hbm.at[pl.ds(0, rows_e // SUBLANES)], xs.at[1 - slot], sem_h.at[1 - slot]).wait()


def _moe_ffn(idx_flat, hx, w_gate, w_up, w_down, layer, rows_e):
    n, wx = hx.shape
    _, n_exp, d, f = w_gate.shape
    fcw = min(256, f)
    n_fc = f // fcw
    mt = math.gcd(rows_e, FFN_MT)
    yt = math.gcd(rows_e, TM)
    assert yt % Y_PAD == 0 and (rows_e // SUBLANES) % (n_fc * (rows_e // mt)) == 0
    kern = functools.partial(_moe_ffn_kernel, rows_e=rows_e, mt=mt, d=d, n_fc=n_fc, n_exp_static=n_exp)
    grid_spec = pltpu.PrefetchScalarGridSpec(
        num_scalar_prefetch=1,
        grid=(n_exp, n_fc),
        in_specs=[
            pl.BlockSpec(memory_space=pl.ANY),
            pl.BlockSpec((None, None, d, fcw), lambda e, c, idx: (layer, e, 0, c)),
            pl.BlockSpec((None, None, d, fcw), lambda e, c, idx: (layer, e, 0, c)),
            pl.BlockSpec((None, None, fcw, d), lambda e, c, idx: (layer, e, c, 0)),
        ],
        out_specs=pl.BlockSpec(memory_space=pl.ANY),
        scratch_shapes=[pltpu.VMEM((2, rows_e // SUBLANES, SUBLANES, wx), U32), pltpu.VMEM((rows_e, d), F32),
                        pltpu.VMEM((2, yt, d + EXT), BF16), pltpu.SemaphoreType.DMA((2,)),
                        pltpu.SemaphoreType.DMA((2,))],
    )
    return pl.pallas_call(
        kern,
        grid_spec=grid_spec,
        out_shape=jax.ShapeDtypeStruct((n_exp * rows_e + Y_PAD, d + EXT), BF16),
        compiler_params=_cparams("arbitrary", "arbitrary", vmem=FFN_VMEM_LIMIT),
        name="moe_ffn",
    )(idx_flat, hx.reshape(n // SUBLANES, SUBLANES, wx), w_gate, w_up, w_down)


def _moe_combine_kernel(toff_ref, y_hbm, x1_ref, gate_ref, o_ref, stage, acc, sem, *, n_exp, rows_e, n_tiles, d):
    T = pl.program_id(0)
    n_grp = n_exp // COMBINE_GROUP
    cpk = 2 * LANES // Y_PAD
    cap = stage.shape[1]

    def plan(tq, grp):
        out, total = [], 0
        for k in range(COMBINE_GROUP):
            e = grp * COMBINE_GROUP + k
            lo = toff_ref[e * (n_tiles + 1) + tq]
            hi = toff_ref[e * (n_tiles + 1) + tq + 1]
            c0 = lo // Y_PAD
            cnt = jnp.where(hi > lo, (hi + Y_PAD - 1) // Y_PAD - c0, 0)
            out.append((e * (rows_e // Y_PAD) + c0, cnt, total))
            total = total + cnt
        return out, total

    def issue(tq, grp, b):
        for src, cnt, pos in plan(tq, grp)[0]:
            def body(c, carry):
                pltpu.make_async_copy(y_hbm.at[src + c], stage.at[b, pos + c], sem.at[b]).start()
                return carry
            lax.fori_loop(0, cnt, body, 0)

    def wait(total, b):
        def body(c, carry):
            pltpu.make_async_copy(y_hbm.at[0], stage.at[b, 0], sem.at[b]).wait()
            return carry
        lax.fori_loop(0, total, body, 0)

    ahead = n_grp - 1

    @pl.when(T == 0)
    def _():
        stage[...] = jnp.zeros(stage.shape, BF16)
        for grp in range(ahead):
            issue(0, grp, grp)

    acc[...] = jnp.zeros(acc.shape, F32)
    tok = (T * TM + lax.broadcasted_iota(I32, (TM, 2 * LANES), 0)).astype(F32)
    for grp in range(n_grp):
        nxt = grp + ahead
        if nxt < n_grp:
            issue(T, nxt, nxt)
        else:
            @pl.when(T + 1 < n_tiles)
            def _():
                issue(T + 1, nxt - n_grp, nxt - n_grp)
        total = plan(T, grp)[1]
        wait(total, grp)

        def block(kc, carry):
            z = stage[grp, pl.ds(kc * cpk, cpk)].reshape(2 * LANES, d + EXT)
            tag = z[:, d:].astype(F32).T
            ztok = tag[0:1, :] * float(CHUNK) + tag[1:2, :]
            zrow = kc * 2 * LANES + lax.broadcasted_iota(I32, (1, 2 * LANES), 1)
            onehot = ((tok == ztok) & (zrow < total * Y_PAD)).astype(BF16)
            acc[...] += _dot(onehot, z[:, :d])
            return carry
        lax.fori_loop(0, (total + cpk - 1) // cpk, block, 0)
    o_ref[...] = x1_ref[...] + gate_ref[...] * acc[...]


def _moe_combine(rows, toff_flat, y, x1, mods, layer, n_exp, rows_e):
    n, d = x1.shape
    wy = y.shape[1]
    assert n_exp % COMBINE_GROUP == 0 and n_exp // COMBINE_GROUP >= 2 and rows_e % Y_PAD == 0
    per_exp = (TM + 2 * (Y_PAD - 1)) // Y_PAD + 1
    cpk = 2 * LANES // Y_PAD
    cap = -(-COMBINE_GROUP * per_exp // cpk) * cpk
    kern = functools.partial(_moe_combine_kernel, n_exp=n_exp, rows_e=rows_e, n_tiles=rows.tiles, d=d)
    grid_spec = pltpu.PrefetchScalarGridSpec(
        num_scalar_prefetch=1,
        grid=(rows.tiles,),
        in_specs=[pl.BlockSpec(memory_space=pl.ANY),
                  pl.BlockSpec((TM, d), lambda i, t: (i, 0)),
                  pl.BlockSpec((None, 1, d), lambda i, t: ((layer * N_MOD + N_MOD - 1) * 8 + rows.mod_row(i), 0, 0))],
        out_specs=pl.BlockSpec((TM, d), lambda i, t: (i, 0)),
        scratch_shapes=[pltpu.VMEM((n_exp // COMBINE_GROUP, cap, Y_PAD, wy), BF16), pltpu.VMEM((TM, d), F32),
                        pltpu.SemaphoreType.DMA((n_exp // COMBINE_GROUP,))],
    )
    return pl.pallas_call(
        kern,
        grid_spec=grid_spec,
        out_shape=jax.ShapeDtypeStruct((n, d), F32),
        compiler_params=_cparams("arbitrary"),
        name="moe_combine",
    )(toff_flat, y.reshape(y.shape[0] // Y_PAD, Y_PAD, wy), x1, mods)


def _final_norm_kernel(x_ref, w_ref, o_ref):
    o_ref[...] = _rms(x_ref[...], w_ref[...])


def _final_norm(x, w, row0, n_rows):
    d = x.shape[1]
    t0 = row0 // TM
    return pl.pallas_call(
        _final_norm_kernel,
        grid=(n_rows // TM,),
        in_specs=[pl.BlockSpec((TM, d), lambda i: (t0 + i, 0)), _full_spec((1, d))],
        out_specs=_row_spec(d),
        out_shape=jax.ShapeDtypeStruct((n_rows, d), F32),
        compiler_params=_cparams("arbitrary"),
        name="final_norm",
    )(x, w.reshape(1, d))


def _rope_tables(seq):
    rows = seq // GRID_W
    row = jnp.repeat(jnp.arange(rows), GRID_W).astype(F32)
    col = jnp.tile(jnp.arange(GRID_W), rows).astype(F32)
    nf = MLA_ROPE // 4
    inv = ROPE_BASE ** (-jnp.arange(nf, dtype=F32) / nf)
    ang = jnp.concatenate([row[:, None] * inv[None], col[:, None] * inv[None]], axis=-1)
    c, s = jnp.cos(ang), jnp.sin(ang)
    cos = jnp.concatenate([c, c, c, c], axis=-1)
    sin = jnp.concatenate([-s, s, -s, s], axis=-1)
    cos = jnp.concatenate([jnp.ones((ROPE_IDENT, LANES), F32), cos], axis=0)
    sin = jnp.concatenate([jnp.zeros((ROPE_IDENT, LANES), F32), sin], axis=0)
    return cos, sin


def kernel(x_prompt, x_sample, cache_ckv, cache_kpe, cache_k, cache_v, c, c_ctx, ada_w, ada_b, norm_mix, norm_ffn,
           mla_w_in, mla_q_norm, mla_w_uq, mla_kv_norm, mla_w_ukv, mla_w_o, swa_w_qkv, swa_sink, swa_w_o,
           moe_router, moe_w_gate, moe_w_up, moe_w_down, final_norm):
    assert MLA_ROPE == SWA_HEAD_DIM
    batch, seq, d = x_prompt.shape
    dec_batch, dec_seq, _ = x_sample.shape
    depth = ada_w.shape[0]
    n_p, n_s = batch * seq, dec_batch * dec_seq
    rows = _Rows(n_p, n_s, dec_seq)
    rows_big = _Rows(n_p, n_s, dec_seq, PROJ_TILE)
    assert seq == TM and n_p % dec_seq == 0
    mla_heads = mla_w_uq.shape[-1] // (MLA_NOPE + MLA_ROPE)
    q_lora, kv_lora = mla_q_norm.shape[-1], mla_kv_norm.shape[-1]
    hq, hkv = swa_sink.shape[-1], cache_k.shape[3]
    n_exp = moe_router.shape[-1]
    cap_p, cap_s = EC_CAPACITY * n_p // n_exp, EC_CAPACITY * n_s // n_exp
    groups = ((0, n_p // CHUNK, cap_p), (n_p // CHUNK, n_s // CHUNK, cap_s))

    cv = jnp.zeros((8, d), F32).at[0].set(c_ctx).at[1:1 + dec_batch].set(c)
    mods = _adaln(cv, ada_w, ada_b)
    mods_r = mods.reshape(depth, 8, N_MOD, d).transpose(0, 2, 1, 3)
    mods_rows = mods_r.reshape(depth * N_MOD * 8, 1, d)
    cos_t, sin_t = _rope_tables(dec_seq)

    xs = (x_prompt.reshape(n_p, d), x_sample.reshape(n_s, d))
    new_ckv, new_kpe, new_k, new_v = [], [], [], []
    for i in range(depth):
        j = i // 2
        if i % 2 == 0:
            w_in = mla_w_in[j]
            w_in_p = jnp.pad(w_in, ((0, 0), (0, q_lora + kv_lora + LANES - w_in.shape[1]))).astype(BF16)
            w_uq = mla_w_uq[j].reshape(q_lora, mla_heads, MLA_NOPE + MLA_ROPE)
            w_uq_p = jnp.pad(w_uq, ((0, 0), (0, 0), (0, 2 * LANES - MLA_NOPE - MLA_ROPE)))
            w_uq_p = w_uq_p.reshape(q_lora, mla_heads * 2 * LANES).astype(BF16)
            q, kvl = _mla_proj(rows_big, xs, mods_rows, i, norm_mix[i], w_in_p, mla_q_norm[j], w_uq_p,
                               mla_kv_norm[j], cos_t, sin_t)
            w_ukv_b = mla_w_ukv[j].astype(BF16)
            o_p = _mla_attn(q, kvl, w_ukv_b, 0, batch, seq, mla_heads)
            o_s = _mla_attn(q, kvl, w_ukv_b, n_p, dec_batch, dec_seq, mla_heads, ctx=(cache_ckv, cache_kpe, j))
            new_ckv.append(kvl[:n_p, :kv_lora].reshape(batch, seq, kv_lora))
            new_kpe.append(kvl[:n_p, kv_lora:kv_lora + MLA_ROPE].reshape(batch, seq, MLA_ROPE))
            w_o_b = mla_w_o[j].astype(BF16)
        else:
            nq, nk = hq * SWA_HEAD_DIM, hkv * SWA_HEAD_DIM
            q, kv = _swa_proj(rows, xs[0], mods_rows, i, norm_mix[i], swa_w_qkv[j].astype(BF16), nq, nk,
                              cos_t, sin_t)
            o_p = _swa_ctx(q, kv, swa_sink[j], batch, seq, hq, hkv)
            ck = cache_k.reshape(cache_k.shape[:3] + (nk,))
            cvv = cache_v.reshape(cache_v.shape[:3] + (nk,))
            o_s = _swa_lat(q, kv, swa_sink[j], ck, cvv, j, n_p, dec_batch, dec_seq, hq, hkv)
            new_k.append(kv[:n_p, :nk].reshape(batch, seq, hkv, SWA_HEAD_DIM))
            new_v.append(kv[:n_p, nk:].reshape(batch, seq, hkv, SWA_HEAD_DIM))
            w_o_b = swa_w_o[j].astype(BF16)
        w_r = moe_router[i]
        w_router_p = jnp.pad(w_r, ((0, 0), (0, EXT - n_exp))).astype(BF16)
        x1, hx, a2 = _attn_out(rows_big, o_p, o_s, xs, w_o_b, mods_rows, i, norm_ffn[i], w_router_p,
                               w_r.T.astype(BF16))
        idx, toff = _moe_select(a2, n_exp, groups)
        rows_e = cap_p + cap_s
        toff = jnp.concatenate([toff[:, :rows.ptiles], toff[:, LANES:LANES + rows.tiles - rows.ptiles],
                                jnp.full((n_exp, 1), rows_e, I32)], axis=1)
        y = _moe_ffn(idx.reshape(-1), hx, moe_w_gate, moe_w_up, moe_w_down, i, rows_e)
        xs = (_moe_combine(rows, toff.reshape(-1), y, x1, mods_rows, i, n_exp, rows_e),)
    x = xs[0]
    y_prompt = _final_norm(x, final_norm, 0, n_p).reshape(batch, seq, d)
    y_sample = _final_norm(x, final_norm, n_p, n_s).reshape(dec_batch, dec_seq, d)
    return (y_prompt, y_sample, jnp.stack(new_ckv, axis=1), jnp.stack(new_kpe, axis=1),
            jnp.stack(new_k, axis=1), jnp.stack(new_v, axis=1))
```
